```python
import math
import jax, jax.numpy as jnp
from jax import lax
import numpy as np

D_MODEL = 4096
BATCH = 2
SEQ = 4096
DEPTH = 4
DEC_BATCH = 1
DEC_SEQ = 8192
PAST_LEN = 128

POOL_WINDOWS = (2, 4, 8, 16)
N_POOL_GROUPS = 4
POOL_GROUP = D_MODEL // 8
W_POOL = N_POOL_GROUPS * POOL_GROUP
W_LRU = 5 * D_MODEL // 4
N_LRU_HEADS = 16
LRU_HEAD = W_LRU // N_LRU_HEADS
CONV_WIDTH = 4
LRU_C = 8.0
W_IN = W_POOL + 2 * W_LRU + 2 * D_MODEL
N_EXPERTS = 64
TOP_K = 8
D_EXPERT = D_MODEL // 8
D_SHARED = D_EXPERT
ROUTE_SCALE = 2.5
DEEPNORM_ALPHA = (2.0 * DEPTH) ** 0.25
DEEPNORM_BETA = (8.0 * DEPTH) ** -0.25
LN_EPS = 1e-5

kernel_name = "hybrid_pool_rglru_moe_encoder"


def layer_norm(x, g, b):
    xf = x.astype(jnp.float32)
    mu = jnp.mean(xf, axis=-1, keepdims=True)
    var = jnp.mean(jnp.square(xf - mu), axis=-1, keepdims=True)
    return ((xf - mu) * lax.rsqrt(var + LN_EPS) * g.astype(jnp.float32) + b.astype(jnp.float32)).astype(x.dtype)


def multiscale_pool(z, w_pool, pool_scale):
    S = z.shape[1]
    zf = z.astype(jnp.float32)
    cs = jnp.pad(jnp.cumsum(zf, axis=1), ((0, 0), (1, 0), (0, 0)))
    t = jnp.arange(S)
    outs = []
    for g, w in enumerate(POOL_WINDOWS):
        lo = jnp.clip(t - w // 2, 0, S)
        hi = jnp.clip(t + w // 2, 0, S)
        csg = cs[..., g * POOL_GROUP:(g + 1) * POOL_GROUP]
        win_sum = jnp.take(csg, hi, axis=1) - jnp.take(csg, lo, axis=1)
        cnt = (hi - lo).astype(jnp.float32)[None, :, None]
        outs.append(win_sum / cnt - zf[..., g * POOL_GROUP:(g + 1) * POOL_GROUP])
    d = jnp.stack(outs, axis=2).astype(z.dtype)
    y = jnp.einsum("bsgp,gpq->bsgq", d, w_pool)
    return y.reshape(z.shape) * pool_scale


def centred_dwconv(x, w, b):
    S = x.shape[1]
    left = CONV_WIDTH // 2
    xp = jnp.pad(x, ((0, 0), (left, CONV_WIDTH - 1 - left), (0, 0)))
    out = xp[:, 0:S] * w[0]
    for k in range(1, CONV_WIDTH):
        out = out + xp[:, k:k + S] * w[k]
    return out + b


def _lin_combine(p, q):
    a1, b1 = p
    a2, b2 = q
    return a1 * a2, a2 * b1 + b2


def rglru(x, w_a, b_a, w_x, b_x, lam, reverse):
    B, S, W = x.shape
    xh = x.reshape(B, S, N_LRU_HEADS, LRU_HEAD)
    r = jax.nn.sigmoid((jnp.einsum("bshi,hij->bshj", xh, w_a).reshape(B, S, W) + b_a).astype(jnp.float32))
    i = jax.nn.sigmoid((jnp.einsum("bshi,hij->bshj", xh, w_x).reshape(B, S, W) + b_x).astype(jnp.float32))
    log_a = -LRU_C * r * jax.nn.softplus(-lam.astype(jnp.float32))
    a = jnp.exp(log_a)
    u = jnp.sqrt(-jnp.expm1(2.0 * log_a)) * (i * x.astype(jnp.float32))
    _, h = lax.associative_scan(_lin_combine, (a, u), axis=1, reverse=reverse)
    return h


def token_mixer(u, w_in, w_pool, pool_scale, conv_w, conv_b, w_rg_a, b_rg_a, w_rg_x, b_rg_x,
                lru_lambda, w_proj_a, w_proj_b, w_out):
    proj = u @ w_in
    z_pool = proj[..., :W_POOL]
    z_lru = proj[..., W_POOL:W_POOL + W_LRU]
    z_gelu = proj[..., W_POOL + W_LRU:W_POOL + 2 * W_LRU]
    g_logits = proj[..., W_POOL + 2 * W_LRU:]
    y_pool = multiscale_pool(z_pool, w_pool, pool_scale)
    xc = centred_dwconv(z_lru, conv_w, conv_b)
    h = (rglru(xc, w_rg_a[0], b_rg_a[0], w_rg_x[0], b_rg_x[0], lru_lambda[0], False)
         + rglru(xc, w_rg_a[1], b_rg_a[1], w_rg_x[1], b_rg_x[1], lru_lambda[1], True))
    y_lru = (jax.nn.gelu(z_gelu.astype(jnp.float32)) * h).astype(u.dtype)
    g_pool = jax.nn.sigmoid(g_logits[..., :D_MODEL])
    g_lru = jax.nn.sigmoid(g_logits[..., D_MODEL:])
    merged = g_pool * (y_pool @ w_proj_a) + g_lru * (y_lru @ w_proj_b)
    return merged @ w_out


def moe(u, w_router, b_router, w_e_gate, w_e_up, w_e_down, w_s_gate, w_s_up, w_s_down):
    B, S, D = u.shape
    t = u.reshape(B * S, D)
    T = t.shape[0]
    scores = jax.nn.sigmoid((t @ w_router).astype(jnp.float32))
    _, idx = lax.top_k(scores + b_router.astype(jnp.float32), TOP_K)
    sel = jnp.take_along_axis(scores, idx, axis=-1)
    wts = sel / jnp.sum(sel, axis=-1, keepdims=True) * ROUTE_SCALE
    gates = jnp.zeros_like(scores).at[jnp.arange(T)[:, None], idx].set(wts).astype(u.dtype)
    hg = jnp.einsum("td,edf->tef", t, w_e_gate)
    hu = jnp.einsum("td,edf->tef", t, w_e_up)
    h = jax.nn.silu(hg) * hu * gates[..., None]
    routed = jnp.einsum("tef,efd->td", h, w_e_down)
    shared = (jax.nn.silu(t @ w_s_gate) * (t @ w_s_up)) @ w_s_down
    return (routed + shared).reshape(B, S, D)


def trunk(x, c, p):
    x = layer_norm(x, p["ln0_g"], p["ln0_b"])
    sc = jax.nn.silu(c)
    for l in range(DEPTH):
        mod = sc @ p["w_ada"][l] + p["b_ada"][l]
        sh1, sc1, g1, sh2, sc2, g2 = jnp.split(mod[:, None, :], 6, axis=-1)
        u = x * (1.0 + sc1) + sh1
        mix = token_mixer(u, p["w_in"][l], p["w_pool"][l], p["pool_scale"][l], p["conv_w"][l], p["conv_b"][l],
                          p["w_rg_a"][l], p["b_rg_a"][l], p["w_rg_x"][l], p["b_rg_x"][l], p["lru_lambda"][l],
                          p["w_proj_a"][l], p["w_proj_b"][l], p["w_out"][l])
        x = layer_norm(DEEPNORM_ALPHA * x + g1 * mix, p["ln1_g"][l], p["ln1_b"][l])
        u = x * (1.0 + sc2) + sh2
        ff = moe(u, p["w_router"][l], p["b_router"][l], p["w_e_gate"][l], p["w_e_up"][l], p["w_e_down"][l],
                 p["w_s_gate"][l], p["w_s_up"][l], p["w_s_down"][l])
        x = layer_norm(DEEPNORM_ALPHA * x + g2 * ff, p["ln2_g"][l], p["ln2_b"][l])
    return x


def setup_inputs(seed: int = 0) -> dict:
    key = jax.random.key(seed)
    ks = jax.random.split(key, 40)
    f32 = jnp.float32
    n = lambda k, shape, s: jax.random.normal(k, shape, f32) * s
    L = DEPTH
    D = D_MODEL
    p_dec = jax.random.uniform(ks[20], (L, 2, W_LRU), f32, 0.9, 0.999)
    a0 = p_dec ** (1.0 / LRU_C)
    return {
        "x_prompt": n(ks[0], (BATCH, SEQ, D), 1.0),
        "x_sample": n(ks[1], (DEC_BATCH, DEC_SEQ, D), 1.0),
        "c_prompt": n(ks[2], (BATCH, D), 1.0),
        "c_sample": n(ks[3], (DEC_BATCH, D), 1.0),
        "ln0_g": 1.0 + n(ks[4], (D,), 0.02),
        "ln0_b": n(ks[5], (D,), 0.02),
        "w_ada": n(ks[6], (L, D, 6 * D), 0.5 * D ** -0.5),
        "b_ada": n(ks[7], (L, 6 * D), 0.02),
        "w_in": n(ks[8], (L, D, W_IN), D ** -0.5),
        "w_pool": n(ks[9], (L, N_POOL_GROUPS, POOL_GROUP, POOL_GROUP), POOL_GROUP ** -0.5),
        "pool_scale": 1.0 + n(ks[10], (L, W_POOL), 0.02),
        "conv_w": n(ks[11], (L, CONV_WIDTH, W_LRU), CONV_WIDTH ** -0.5),
        "conv_b": n(ks[12], (L, W_LRU), 0.02),
        "w_rg_a": n(ks[13], (L, 2, N_LRU_HEADS, LRU_HEAD, LRU_HEAD), LRU_HEAD ** -0.5),
        "b_rg_a": n(ks[14], (L, 2, W_LRU), 0.02),
        "w_rg_x": n(ks[15], (L, 2, N_LRU_HEADS, LRU_HEAD, LRU_HEAD), LRU_HEAD ** -0.5),
        "b_rg_x": n(ks[16], (L, 2, W_LRU), 0.02),
        "lru_lambda": jnp.log(a0) - jnp.log1p(-a0),
        "w_proj_a": n(ks[17], (L, W_POOL, D), W_POOL ** -0.5),
        "w_proj_b": n(ks[18], (L, W_LRU, D), W_LRU ** -0.5),
        "w_out": n(ks[19], (L, D, D), DEEPNORM_BETA * D ** -0.5),
        "ln1_g": 1.0 + n(ks[21], (L, D), 0.02),
        "ln1_b": n(ks[22], (L, D), 0.02),
        "w_router": n(ks[23], (L, D, N_EXPERTS), D ** -0.5),
        "b_router": n(ks[24], (L, N_EXPERTS), 0.01),
        "w_e_gate": n(ks[25], (L, N_EXPERTS, D, D_EXPERT), D ** -0.5),
        "w_e_up": n(ks[26], (L, N_EXPERTS, D, D_EXPERT), D ** -0.5),
        "w_e_down": n(ks[27], (L, N_EXPERTS, D_EXPERT, D), DEEPNORM_BETA * D_EXPERT ** -0.5),
        "w_s_gate": n(ks[28], (L, D, D_SHARED), D ** -0.5),
        "w_s_up": n(ks[29], (L, D, D_SHARED), D ** -0.5),
        "w_s_down": n(ks[30], (L, D_SHARED, D), DEEPNORM_BETA * D_SHARED ** -0.5),
        "ln2_g": 1.0 + n(ks[31], (L, D), 0.02),
        "ln2_b": n(ks[32], (L, D), 0.02),
    }


def reference(x_prompt, x_sample, c_prompt, c_sample, ln0_g, ln0_b, w_ada, b_ada, w_in, w_pool, pool_scale,
              conv_w, conv_b, w_rg_a, b_rg_a, w_rg_x, b_rg_x, lru_lambda, w_proj_a, w_proj_b, w_out,
              ln1_g, ln1_b, w_router, b_router, w_e_gate, w_e_up, w_e_down, w_s_gate, w_s_up, w_s_down,
              ln2_g, ln2_b):
    p = dict(ln0_g=ln0_g, ln0_b=ln0_b, w_ada=w_ada, b_ada=b_ada, w_in=w_in, w_pool=w_pool,
             pool_scale=pool_scale, conv_w=conv_w, conv_b=conv_b, w_rg_a=w_rg_a, b_rg_a=b_rg_a,
             w_rg_x=w_rg_x, b_rg_x=b_rg_x, lru_lambda=lru_lambda, w_proj_a=w_proj_a, w_proj_b=w_proj_b,
             w_out=w_out, ln1_g=ln1_g, ln1_b=ln1_b, w_router=w_router, b_router=b_router,
             w_e_gate=w_e_gate, w_e_up=w_e_up, w_e_down=w_e_down, w_s_gate=w_s_gate, w_s_up=w_s_up,
             w_s_down=w_s_down, ln2_g=ln2_g, ln2_b=ln2_b)
    y_prompt = trunk(x_prompt, c_prompt, p)
    y_sample = trunk(x_sample, c_sample, p)
    return (y_prompt, y_sample)
```

```python
import functools
import math

import jax
import jax.numpy as jnp
from jax import lax
from jax.experimental import pallas as pl
from jax.experimental.pallas import tpu as pltpu

POOL_WINDOWS = (2, 4, 8, 16)
CONV_WIDTH = 4
LRU_C = 8.0
TOP_K = 8
ROUTE_SCALE = 2.5
LN_EPS = 1e-5
HALO = 8
VMEM_LIMIT_V7X = 56 * 1024 * 1024

f32 = jnp.float32
bf16 = jnp.bfloat16
i32 = jnp.int32


def _tile(n, pref):
    if n <= pref:
        return n
    t = pref - pref % 8
    while t >= 8:
        if n % t == 0:
            return t
        t -= 8
    return n


def _params(*sem):
    return pltpu.CompilerParams(dimension_semantics=sem, vmem_limit_bytes=VMEM_LIMIT_V7X)


def _layer_norm(v, w, b):
    mu = jnp.mean(v, axis=-1, keepdims=True)
    d = v - mu
    var = jnp.mean(d * d, axis=-1, keepdims=True)
    return d * lax.rsqrt(var + LN_EPS) * w + b


def _ada_kernel(c_ref, w_ref, b_ref, o_ref):
    c = c_ref[...]
    s = c * jax.nn.sigmoid(c)
    o_ref[...] = jnp.dot(s.astype(bf16), w_ref[...].astype(bf16),
                         preferred_element_type=f32) + b_ref[...]


def _ada(c8, w_ada, b_ada):
    L, D, N = w_ada.shape
    tn = _tile(N, 512)
    return pl.pallas_call(
        _ada_kernel,
        out_shape=jax.ShapeDtypeStruct((L, 8, N), f32),
        grid=(L, N // tn),
        in_specs=[pl.BlockSpec((8, D), lambda l, j: (0, 0)),
                  pl.BlockSpec((None, D, tn), lambda l, j: (l, 0, j)),
                  pl.BlockSpec((None, 1, tn), lambda l, j: (l, 0, j))],
        out_specs=pl.BlockSpec((None, 8, tn), lambda l, j: (l, 0, j)),
        compiler_params=_params("parallel", "parallel"),
        name="ada_mod",
    )(c8, w_ada, b_ada.reshape(L, 1, N))


def _ln_first_kernel(x_ref, w_ref, b_ref, sc_ref, sh_ref, xo_ref, uo_ref):
    y = _layer_norm(x_ref[...], w_ref[...], b_ref[...])
    xo_ref[...] = y
    uo_ref[...] = (y * (1.0 + sc_ref[...]) + sh_ref[...]).astype(uo_ref.dtype)


def _ln_mid_kernel(x_ref, d_ref, g_ref, w_ref, b_ref, sc_ref, sh_ref, *out_refs, alpha):
    y = _layer_norm(alpha * x_ref[...] + g_ref[...] * d_ref[...], w_ref[...], b_ref[...])
    out_refs[0][...] = y
    u = y * (1.0 + sc_ref[...]) + sh_ref[...]
    for r in out_refs[1:]:
        r[...] = u.astype(r.dtype)


def _combine_kernel(x_ref, s_ref, g_ref, w_ref, b_ref, *rest, alpha, last):
    e_refs, rest = rest[:TOP_K], rest[TOP_K:]
    ff = e_refs[0][...]
    for e_ref in e_refs[1:]:
        ff = ff + e_ref[...]
    ff = ff + s_ref[...]
    y = _layer_norm(alpha * x_ref[...] + g_ref[...] * ff, w_ref[...], b_ref[...])
    if last:
        rest[0][...] = y
    else:
        sc_ref, sh_ref, xo_ref, uo_ref = rest
        xo_ref[...] = y
        uo_ref[...] = (y * (1.0 + sc_ref[...]) + sh_ref[...]).astype(uo_ref.dtype)


def _mod_block_spec(D, l, which, blocks_per_tile_fn):
    return pl.BlockSpec((None, None, None, 1, D), lambda i: (l, blocks_per_tile_fn(i), which, 0, 0))


def _ln_first(x, w, b, mod5, l, sb):
    T, D = x.shape
    tm = _tile(sb, 128)
    blk = lambda i: (i * tm) // sb
    row = pl.BlockSpec((tm, D), lambda i: (i, 0))
    vec = pl.BlockSpec((1, D), lambda i: (0, 0))
    return pl.pallas_call(
        _ln_first_kernel,
        out_shape=(jax.ShapeDtypeStruct((T, D), f32), jax.ShapeDtypeStruct((T, D), bf16)),
        grid=(T // tm,),
        in_specs=[row, vec, vec, _mod_block_spec(D, l, 1, blk), _mod_block_spec(D, l, 0, blk)],
        out_specs=(row, row),
        compiler_params=_params("parallel"),
        name="ln_first",
    )(x, w.reshape(1, D), b.reshape(1, D), mod5, mod5)


def _ln_mid(x, delta, w, b, mod5, l, sb, alpha, g_idx, sc_idx, sh_idx, with_f32):
    T, D = x.shape
    tm = _tile(sb, 128)
    blk = lambda i: (i * tm) // sb
    row = pl.BlockSpec((tm, D), lambda i: (i, 0))
    vec = pl.BlockSpec((None, 1, D), lambda i: (l, 0, 0))
    outs = [jax.ShapeDtypeStruct((T, D), f32), jax.ShapeDtypeStruct((T, D), bf16)]
    if with_f32:
        outs.append(jax.ShapeDtypeStruct((T, D), f32))
    return pl.pallas_call(
        functools.partial(_ln_mid_kernel, alpha=alpha),
        out_shape=tuple(outs),
        grid=(T // tm,),
        in_specs=[row, row, _mod_block_spec(D, l, g_idx, blk), vec, vec,
                  _mod_block_spec(D, l, sc_idx, blk), _mod_block_spec(D, l, sh_idx, blk)],
        out_specs=tuple(row for _ in outs),
        compiler_params=_params("parallel"),
        name="ln_mid",
    )(x, delta, mod5, w, b, mod5, mod5)


def _combine(x, eo, shared, w, b, mod5, l, sb, alpha, last):
    T, D = x.shape
    tm = _tile(sb, 64)
    assert eo.shape[0] % tm == 0
    blk = lambda i: (i * tm) // sb
    row = pl.BlockSpec((tm, D), lambda i: (i, 0))
    vec = pl.BlockSpec((None, 1, D), lambda i: (l, 0, 0))
    in_specs = [row, row, _mod_block_spec(D, l, 5, blk), vec, vec]
    in_specs += [pl.BlockSpec((tm, D), lambda i, k=k: (k * (T // tm) + i, 0)) for k in range(TOP_K)]
    args = [x, shared, mod5, w, b] + [eo] * TOP_K
    if last:
        out_shape = jax.ShapeDtypeStruct((T, D), f32)
        out_specs = row
    else:
        in_specs += [_mod_block_spec(D, l + 1, 1, blk), _mod_block_spec(D, l + 1, 0, blk)]
        args += [mod5, mod5]
        out_shape = (jax.ShapeDtypeStruct((T, D), f32), jax.ShapeDtypeStruct((T, D), bf16))
        out_specs = (row, row)
    return pl.pallas_call(
        functools.partial(_combine_kernel, alpha=alpha, last=last),
        out_shape=out_shape,
        grid=(T // tm,),
        in_specs=in_specs,
        out_specs=out_specs,
        compiler_params=_params("parallel"),
        name="moe_combine_ln",
    )(*args)


def _mm_kernel(a_ref, b_ref, o_ref, *, act):
    acc = jnp.dot(a_ref[...], b_ref[...], preferred_element_type=f32)
    if act == "gelu":
        acc = jax.nn.gelu(acc)
    elif act == "sigmoid":
        acc = jax.nn.sigmoid(acc)
    o_ref[...] = acc.astype(o_ref.dtype)


def _matmul(a, w, l, col0, ncols, act, out_dtype, tm_pref=1024, tn_pref=1024):
    M, K = a.shape
    tm = _tile(M, tm_pref)
    tn = _tile(math.gcd(ncols, col0) if col0 else ncols, tn_pref)
    off = col0 // tn
    return pl.pallas_call(
        functools.partial(_mm_kernel, act=act),
        out_shape=jax.ShapeDtypeStruct((M, ncols), out_dtype),
        grid=(M // tm, ncols // tn),
        in_specs=[pl.BlockSpec((tm, K), lambda i, j: (i, 0)),
                  pl.BlockSpec((None, K, tn), lambda i, j: (l, 0, j + off))],
        out_specs=pl.BlockSpec((tm, tn), lambda i, j: (i, j)),
        compiler_params=_params("parallel", "parallel"),
        name="matmul_" + (act or "id"),
    )(a, w)


def _merge_kernel(ya_ref, yb_ref, wa_ref, wb_ref, ga_ref, gb_ref, o_ref):
    pa = jnp.dot(ya_ref[...], wa_ref[...], preferred_element_type=f32)
    pb = jnp.dot(yb_ref[...], wb_ref[...], preferred_element_type=f32)
    o_ref[...] = (ga_ref[...].astype(f32) * pa + gb_ref[...].astype(f32) * pb).astype(o_ref.dtype)


def _merge(ya, yb, wa, wb, gates, l):
    T, Ka = ya.shape
    Kb = yb.shape[1]
    D = wa.shape[2]
    tm = _tile(T, 512)
    tn = _tile(D, 512)
    nj = D // tn
    return pl.pallas_call(
        _merge_kernel,
        out_shape=jax.ShapeDtypeStruct((T, D), bf16),
        grid=(T // tm, nj),
        in_specs=[pl.BlockSpec((tm, Ka), lambda i, j: (i, 0)),
                  pl.BlockSpec((tm, Kb), lambda i, j: (i, 0)),
                  pl.BlockSpec((None, Ka, tn), lambda i, j: (l, 0, j)),
                  pl.BlockSpec((None, Kb, tn), lambda i, j: (l, 0, j)),
                  pl.BlockSpec((tm, tn), lambda i, j: (i, j)),
                  pl.BlockSpec((tm, tn), lambda i, j: (i, j + nj))],
        out_specs=pl.BlockSpec((tm, tn), lambda i, j: (i, j)),
        compiler_params=_params("parallel", "parallel"),
        name="merge_proj",
    )(ya, yb, wa, wb, gates, gates)


def _shared_kernel(u_ref, wg_ref, wu_ref, wd_ref, o_ref):
    u = u_ref[...]
    hg = jnp.dot(u, wg_ref[...], preferred_element_type=f32)
    hu = jnp.dot(u, wu_ref[...], preferred_element_type=f32)
    h = (hg * jax.nn.sigmoid(hg)) * hu
    o_ref[...] = jnp.dot(h.astype(bf16), wd_ref[...], preferred_element_type=f32)


def _shared_mlp(u, wg, wu, wd, l):
    T, D = u.shape
    F = wg.shape[2]
    tm = _tile(T, 256)
    return pl.pallas_call(
        _shared_kernel,
        out_shape=jax.ShapeDtypeStruct((T, D), f32),
        grid=(T // tm,),
        in_specs=[pl.BlockSpec((tm, D), lambda i: (i, 0)),
                  pl.BlockSpec((None, D, F), lambda i: (l, 0, 0)),
                  pl.BlockSpec((None, D, F), lambda i: (l, 0, 0)),
                  pl.BlockSpec((None, F, D), lambda i: (l, 0, 0))],
        out_specs=pl.BlockSpec((tm, D), lambda i: (i, 0)),
        compiler_params=_params("parallel"),
        name="shared_mlp",
    )(u, wg, wu, wd)


def _with_halo(z_ref, zp_ref, zn_ref, first, last):
    zp = jnp.where(first, 0.0, zp_ref[...])
    zn = jnp.where(last, 0.0, zn_ref[...])
    return jnp.concatenate([zp, z_ref[...], zn], axis=0)


def _pool_kernel(pos_ref, len_ref, z_ref, zp_ref, zn_ref, w_ref, s_ref, o_ref, *, tq):
    i = pl.program_id(0)
    pos0 = pos_ref[i]
    slen = len_ref[i]
    e = _with_halo(z_ref, zp_ref, zn_ref, pos0 == 0, pos0 + tq == slen)
    n = tq + 2 * HALO
    P = z_ref.shape[1] // len(POOL_WINDOWS)
    pos = pos0 + lax.broadcasted_iota(i32, (tq, 1), 0)
    for g, w in enumerate(POOL_WINDOWS):
        c = e[:, g * P:(g + 1) * P]
        span = 1
        while span < w:
            c = c + pltpu.roll(c, span, axis=0)
            span *= 2
        if w // 2 > 1:
            c = pltpu.roll(c, n - (w // 2 - 1), axis=0)
        s = c[HALO:HALO + tq]
        cnt = (jnp.minimum(pos + w // 2, slen) - jnp.maximum(pos - w // 2, 0)).astype(f32)
        d = s / cnt - z_ref[:, g * P:(g + 1) * P]
        y = jnp.dot(d.astype(bf16), w_ref[g], preferred_element_type=f32)
        o_ref[:, g * P:(g + 1) * P] = (y * s_ref[:, g * P:(g + 1) * P]).astype(o_ref.dtype)


def _halo_specs(tq, C, nrow8, col):
    r = tq // HALO
    return [pl.BlockSpec((tq, C), lambda *a: (col(*a)[0], col(*a)[1])),
            pl.BlockSpec((HALO, C), lambda *a: (jnp.maximum(col(*a)[0] * r - 1, 0), col(*a)[1])),
            pl.BlockSpec((HALO, C), lambda *a: (jnp.minimum((col(*a)[0] + 1) * r, nrow8 - 1), col(*a)[1]))]


def _pool(z, w_pool, pool_scale, l, tile_pos, tile_len, tq):
    T, W = z.shape
    G, P = w_pool.shape[1], w_pool.shape[2]
    grid_spec = pltpu.PrefetchScalarGridSpec(
        num_scalar_prefetch=2,
        grid=(T // tq,),
        in_specs=_halo_specs(tq, W, T // HALO, lambda i, *_: (i, 0)) + [
            pl.BlockSpec((None, G, P, P), lambda i, *_: (l, 0, 0, 0)),
            pl.BlockSpec((None, 1, W), lambda i, *_: (l, 0, 0))],
        out_specs=pl.BlockSpec((tq, W), lambda i, *_: (i, 0)),
    )
    return pl.pallas_call(
        functools.partial(_pool_kernel, tq=tq),
        out_shape=jax.ShapeDtypeStruct((T, W), bf16),
        grid_spec=grid_spec,
        compiler_params=_params("parallel"),
        name="pool_mixer",
    )(tile_pos, tile_len, z, z, z, w_pool, pool_scale)


def _lru_inputs(z_ref, zp_ref, zn_ref, cw_ref, cb_ref, w_ref, b_ref, lam_ref, first, last, tq):
    e = _with_halo(z_ref, zp_ref, zn_ref, first, last)
    n = tq + 2 * HALO
    left = CONV_WIDTH // 2
    acc = None
    for k in range(CONV_WIDTH):
        shift = (left - k) % n
        tap = (pltpu.roll(e, shift, axis=0) if shift else e)[HALO:HALO + tq] * cw_ref[k:k + 1, :]
        acc = tap if acc is None else acc + tap
    xc = acc + cb_ref[...]
    C = xc.shape[1]
    pre = jnp.dot(xc.astype(bf16), w_ref[...], preferred_element_type=f32) + b_ref[...]
    r = jax.nn.sigmoid(pre[:, :C])
    ig = jax.nn.sigmoid(pre[:, C:])
    log_a = (-LRU_C) * r * jax.nn.softplus(-lam_ref[...])
    a = jnp.exp(log_a)
    th = jnp.tanh(log_a)
    u = jnp.sqrt(-2.0 * th / (1.0 - th)) * (ig * xc)
    return a, u


def _scan8(a, u, reverse):
    tq, C = a.shape
    A = a.reshape(tq // 8, 8, C)
    U = u.reshape(tq // 8, 8, C)
    r = lax.broadcasted_iota(i32, (tq // 8, 8, C), 1)
    for s in (1, 2, 4):
        if reverse:
            As, Us, m = pltpu.roll(A, 8 - s, axis=1), pltpu.roll(U, 8 - s, axis=1), r < 8 - s
        else:
            As, Us, m = pltpu.roll(A, s, axis=1), pltpu.roll(U, s, axis=1), r >= s
        U = jnp.where(m, A * Us + U, U)
        A = jnp.where(m, A * As, A)
    return A.reshape(tq, C), U.reshape(tq, C)


def _scan_groups(a_scr, u_scr, h_ref, carry0, tq, reverse):
    ng = tq // 8
    C = a_scr.shape[1]

    def body(g, carry):
        gg = (ng - 1 - g) if reverse else g
        rows = pl.ds(pl.multiple_of(gg * 8, 8), 8)
        h = u_scr[rows, :] + a_scr[rows, :] * carry
        h_ref[rows, :] = h
        edge = h[0:1, :] if reverse else h[7:8, :]
        return jnp.broadcast_to(edge, (8, C))

    return lax.fori_loop(0, ng, body, carry0, unroll=8)


def _lru_fwd_kernel(pos_ref, len_ref, z_ref, zp_ref, zn_ref, cw_ref, cb_ref, w_ref, b_ref, lam_ref,
                    h_ref, carry_ref, a_scr, u_scr, *, tq):
    i = pl.program_id(1)
    pos0 = pos_ref[i]
    first = pos0 == 0
    a, u = _lru_inputs(z_ref, zp_ref, zn_ref, cw_ref, cb_ref, w_ref, b_ref, lam_ref,
                       first, pos0 + tq == len_ref[i], tq)
    a_scr[...], u_scr[...] = _scan8(a, u, False)
    carry0 = jnp.where(first, 0.0, carry_ref[...])
    carry_ref[...] = _scan_groups(a_scr, u_scr, h_ref, carry0, tq, False)


def _lru_bwd_kernel(pos_ref, len_ref, z_ref, zp_ref, zn_ref, cw_ref, cb_ref, w_ref, b_ref, lam_ref,
                    hf_ref, gz_ref, y_ref, carry_ref, a_scr, u_scr, h_scr, *, tq, nt):
    i = nt - 1 - pl.program_id(1)
    pos0 = pos_ref[i]
    last = pos0 + tq == len_ref[i]
    a, u = _lru_inputs(z_ref, zp_ref, zn_ref, cw_ref, cb_ref, w_ref, b_ref, lam_ref,
                       pos0 == 0, last, tq)
    a_scr[...], u_scr[...] = _scan8(a, u, True)
    carry0 = jnp.where(last, 0.0, carry_ref[...])
    carry_ref[...] = _scan_groups(a_scr, u_scr, h_scr, carry0, tq, True)
    y_ref[...] = (gz_ref[...].astype(f32) * (hf_ref[...] + h_scr[...])).astype(y_ref.dtype)


def _lru(z, gz, conv_w, conv_b, w_pair, b_pair, lam_pair, l, tile_pos, tile_len, tq):
    T, W = z.shape
    NP, C = w_pair.shape[1], w_pair.shape[2]
    nt = T // tq
    scratch = [pltpu.VMEM((8, C), f32), pltpu.VMEM((tq, C), f32), pltpu.VMEM((tq, C), f32)]

    def specs(tile, d):
        col = lambda p, i, *_: (tile(i), p)
        return _halo_specs(tq, C, T // HALO, col) + [
            pl.BlockSpec((None, CONV_WIDTH, C), lambda p, i, *_: (l, 0, p)),
            pl.BlockSpec((None, 1, C), lambda p, i, *_: (l, 0, p)),
            pl.BlockSpec((None, None, C, 2 * C), lambda p, i, *_: (l, p, 0, d)),
            pl.BlockSpec((None, None, 1, 2 * C), lambda p, i, *_: (l, p, 0, d)),
            pl.BlockSpec((None, None, 1, C), lambda p, i, *_: (l, p, 0, d))]

    fwd_tile = lambda i: i
    h_fwd = pl.pallas_call(
        functools.partial(_lru_fwd_kernel, tq=tq),
        out_shape=jax.ShapeDtypeStruct((T, W), f32),
        grid_spec=pltpu.PrefetchScalarGridSpec(
            num_scalar_prefetch=2, grid=(NP, nt), in_specs=specs(fwd_tile, 0),
            out_specs=pl.BlockSpec((tq, C), lambda p, i, *_: (i, p)),
            scratch_shapes=scratch),
        compiler_params=_params("parallel", "arbitrary"),
        name="rglru_fwd",
    )(tile_pos, tile_len, z, z, z, conv_w, conv_b, w_pair, b_pair, lam_pair)

    bwd_tile = lambda i: nt - 1 - i
    row = pl.BlockSpec((tq, C), lambda p, i, *_: (nt - 1 - i, p))
    return pl.pallas_call(
        functools.partial(_lru_bwd_kernel, tq=tq, nt=nt),
        out_shape=jax.ShapeDtypeStruct((T, W), bf16),
        grid_spec=pltpu.PrefetchScalarGridSpec(
            num_scalar_prefetch=2, grid=(NP, nt), in_specs=specs(bwd_tile, 1) + [row, row],
            out_specs=row,
            scratch_shapes=scratch + [pltpu.VMEM((tq, C), f32)]),
        compiler_params=_params("parallel", "arbitrary"),
        name="rglru_bwd",
    )(tile_pos, tile_len, z, z, z, conv_w, conv_b, w_pair, b_pair, lam_pair, h_fwd, gz)


def _router_kernel(u_ref, w_ref, b_ref, idx_ref, wts_ref, pos_ref, cnt_ref, carry_ref, *, tm, E):
    @pl.when(pl.program_id(0) == 0)
    def _():
        carry_ref[...] = jnp.zeros_like(carry_ref)

    logits = jnp.dot(u_ref[...], w_ref[...], preferred_element_type=f32)
    scores = jax.nn.sigmoid(logits)
    work = scores + b_ref[...]
    lane = lax.broadcasted_iota(i32, (tm, E), 1).astype(f32)
    lane_k = lax.broadcasted_iota(i32, (tm, TOP_K), 1)
    mask = jnp.zeros((tm, E), f32)
    idx8 = jnp.zeros((tm, TOP_K), f32)
    sel8 = jnp.zeros((tm, TOP_K), f32)
    onehots = []
    for k in range(TOP_K):
        m = jnp.max(work, axis=-1, keepdims=True)
        ik = jnp.min(jnp.where(work == m, lane, float(E)), axis=-1, keepdims=True)
        oh = lane == ik
        sk = jnp.sum(jnp.where(oh, scores, 0.0), axis=-1, keepdims=True)
        work = jnp.where(oh, -jnp.inf, work)
        mask = jnp.where(oh, 1.0, mask)
        idx8 = jnp.where(lane_k == k, ik, idx8)
        sel8 = jnp.where(lane_k == k, sk, sel8)
        onehots.append(oh)
    idx_ref[...] = idx8.astype(i32)
    wts_ref[...] = sel8 / jnp.sum(sel8, axis=-1, keepdims=True) * ROUTE_SCALE

    before = (lax.broadcasted_iota(i32, (tm, tm), 1) < lax.broadcasted_iota(i32, (tm, tm), 0))
    excl = jnp.dot(before.astype(bf16), mask.astype(bf16), preferred_element_type=f32)
    posd = excl + carry_ref[...]
    pos8 = jnp.zeros((tm, TOP_K), f32)
    for k in range(TOP_K):
        pk = jnp.sum(jnp.where(onehots[k], posd, 0.0), axis=-1, keepdims=True)
        pos8 = jnp.where(lane_k == k, pk, pos8)
    pos_ref[...] = pos8.astype(i32)
    total = carry_ref[...] + jnp.sum(mask, axis=0, keepdims=True)
    carry_ref[...] = total
    cnt_ref[...] = total.astype(i32)


def _router(u, w_router, b_router, l):
    T, D = u.shape
    E = w_router.shape[2]
    tm = _tile(T, 256)
    small = lambda n, dt: jax.ShapeDtypeStruct((T, n), dt)
    kspec = pl.BlockSpec((tm, TOP_K), lambda i: (i, 0))
    return pl.pallas_call(
        functools.partial(_router_kernel, tm=tm, E=E),
        out_shape=(small(TOP_K, i32), small(TOP_K, f32), small(TOP_K, i32),
                   jax.ShapeDtypeStruct((1, E), i32)),
        grid=(T // tm,),
        in_specs=[pl.BlockSpec((tm, D), lambda i: (i, 0)),
                  pl.BlockSpec((None, D, E), lambda i: (l, 0, 0)),
                  pl.BlockSpec((None, 1, E), lambda i: (l, 0, 0))],
        out_specs=(kspec, kspec, kspec, pl.BlockSpec((1, E), lambda i: (0, 0))),
        scratch_shapes=[pltpu.VMEM((1, E), f32)],
        compiler_params=_params("arbitrary"),
        name="router_topk",
    )(u, w_router, b_router.reshape(-1, 1, E))


def _expert_kernel(te_ref, rows_ref, tok_ref, tokn_ref, dst_ref, gate_ref, wg_ref, wu_ref, wd_ref,
                   x_hbm, eo_hbm, xbuf, obuf, gsem, ssem, *, tm, nt):
    i = pl.program_id(0)
    slot = i % 2

    def used(j):
        return rows_ref[jnp.clip(j, 0, nt - 1)] > 0

    def gather(idx_ref, s):
        def body(r, c):
            pltpu.make_async_copy(x_hbm.at[pl.ds(idx_ref[0, 0, r], 1)],
                                  xbuf.at[s, pl.ds(r, 1)], gsem.at[s]).start()
            return c
        lax.fori_loop(0, tm, body, 0, unroll=8)

    def wait_gather(s):
        pltpu.make_async_copy(x_hbm.at[pl.ds(0, tm)], xbuf.at[s], gsem.at[s]).wait()

    def wait_scatter(s):
        pltpu.make_async_copy(obuf.at[s], eo_hbm.at[pl.ds(0, tm)], ssem.at[s]).wait()

    @pl.when(i == 0)
    def _():
        obuf[1] = jnp.zeros(obuf.shape[1:], obuf.dtype)
        spare = eo_hbm.shape[0] - 2 * tm
        for half in range(2):
            pltpu.make_async_copy(obuf.at[1], eo_hbm.at[pl.ds(spare + half * tm, tm)], ssem.at[1]).start()
        for half in range(2):
            wait_scatter(1)

    @pl.when(jnp.logical_and(i == 0, used(0)))
    def _():
        gather(tok_ref, 0)

    @pl.when(jnp.logical_and(i + 1 < nt, used(i + 1)))
    def _():
        gather(tokn_ref, 1 - slot)

    @pl.when(used(i))
    def _():
        wait_gather(slot)

    @pl.when(jnp.logical_and(i >= 2, used(i - 2)))
    def _():
        wait_scatter(slot)

    @pl.when(used(i))
    def _():
        x = xbuf[slot].astype(bf16)
        hg = jnp.dot(x, wg_ref[...], preferred_element_type=f32)
        hu = jnp.dot(x, wu_ref[...], preferred_element_type=f32)
        h = (hg * jax.nn.sigmoid(hg)) * hu * gate_ref[...]
        obuf[slot] = jnp.dot(h.astype(bf16), wd_ref[...], preferred_element_type=f32)

        def body(r, c):
            pltpu.make_async_copy(obuf.at[slot, pl.ds(r, 1)],
                                  eo_hbm.at[pl.ds(dst_ref[0, 0, r], 1)], ssem.at[slot]).start()
            return c
        lax.fori_loop(0, tm, body, 0, unroll=8)

    @pl.when(i == nt - 1)
    def _():
        @pl.when(jnp.logical_and(i >= 1, used(i - 1)))
        def _():
            wait_scatter(1 - slot)

        @pl.when(used(i))
        def _():
            wait_scatter(slot)


def _experts(u_f32, wg, wu, wd, l, tile_expert, tile_rows, tok_of_slot, dst_of_slot, gate_of_slot, tm):
    T, D = u_f32.shape
    F = wg.shape[3]
    nt = tile_expert.shape[0]
    smem_tile = lambda f: pl.BlockSpec((1, 1, tm), f, memory_space=pltpu.SMEM)
    wspec = lambda shape: pl.BlockSpec((None, None) + shape, lambda i, te, rows: (l, te[i], 0, 0))
    grid_spec = pltpu.PrefetchScalarGridSpec(
        num_scalar_prefetch=2,
        grid=(nt,),
        in_specs=[smem_tile(lambda i, te, rows: (i, 0, 0)),
                  smem_tile(lambda i, te, rows: (jnp.minimum(i + 1, nt - 1), 0, 0)),
                  smem_tile(lambda i, te, rows: (i, 0, 0)),
                  pl.BlockSpec((tm, 1), lambda i, te, rows: (i, 0)),
                  wspec((D, F)), wspec((D, F)), wspec((F, D)),
                  pl.BlockSpec(memory_space=pl.ANY)],
        out_specs=pl.BlockSpec(memory_space=pl.ANY),
        scratch_shapes=[pltpu.VMEM((2, tm, D), f32), pltpu.VMEM((2, tm, D), f32),
                        pltpu.SemaphoreType.DMA((2,)), pltpu.SemaphoreType.DMA((2,))],
    )
    tok3 = tok_of_slot.reshape(nt, 1, tm)
    return pl.pallas_call(
        functools.partial(_expert_kernel, tm=tm, nt=nt),
        out_shape=jax.ShapeDtypeStruct((TOP_K * T + 2 * tm, D), f32),
        grid_spec=grid_spec,
        compiler_params=_params("arbitrary"),
        name="expert_mlp",
    )(tile_expert, tile_rows, tok3, tok3, dst_of_slot.reshape(nt, 1, tm),
      gate_of_slot.reshape(nt * tm, 1), wg, wu, wd, u_f32)


def _dispatch(idx8, wts8, pos8, counts, tm, nt):
    T, K = idx8.shape
    E = counts.shape[0]
    tiles_e = (counts + tm - 1) // tm
    tile_end = jnp.cumsum(tiles_e)
    tile_start = tile_end - tiles_e
    slot = (tile_start * tm)[idx8] + pos8
    flat = slot.reshape(-1)
    tok = jnp.broadcast_to(jnp.arange(T, dtype=i32)[:, None], (T, K)).reshape(-1)
    dst = (jnp.arange(K, dtype=i32)[None, :] * T + jnp.arange(T, dtype=i32)[:, None]).reshape(-1)
    scatter = lambda init, v: init.at[flat].set(v, unique_indices=True)
    tok_of_slot = scatter(jnp.zeros((nt * tm,), i32), tok)
    sid = jnp.arange(nt * tm, dtype=i32)
    dst_of_slot = scatter(K * T + (sid // tm) % 2 * tm + sid % tm, dst)
    gate_of_slot = scatter(jnp.zeros((nt * tm,), f32), wts8.reshape(-1))
    tid = jnp.arange(nt, dtype=i32)
    total = tile_end[-1]
    te = jnp.minimum(jnp.searchsorted(tile_end, jnp.minimum(tid, total - 1), side="right"), E - 1).astype(i32)
    rows = jnp.clip(counts[te] - (tid - tile_start[te]) * tm, 0, tm)
    rows = jnp.where(tid < total, rows, 0).astype(i32)
    return te, rows, tok_of_slot, dst_of_slot, gate_of_slot


def _pair_heads(w_a, w_x, b_a, b_x, lam):
    L, _, H, hd, _ = w_a.shape
    NP = H // 2
    w = jnp.stack([w_a, w_x], axis=2).reshape(L, 2, 2, NP, 2, hd, hd)
    w = jnp.einsum("ldgphij,hk->lphidgkj", w, jnp.eye(2, dtype=w.dtype))
    w = w.reshape(L, NP, 2 * hd, 8 * hd).astype(bf16)
    b = jnp.stack([b_a, b_x], axis=2).reshape(L, 2, 2, NP, 2 * hd)
    b = b.transpose(0, 3, 1, 2, 4).reshape(L, NP, 1, 8 * hd)
    lam = lam.reshape(L, 2, NP, 2 * hd).transpose(0, 2, 1, 3).reshape(L, NP, 1, 4 * hd)
    return w, b, lam


def kernel(x_prompt, x_sample, c_prompt, c_sample, ln0_g, ln0_b, w_ada, b_ada, w_in, w_pool, pool_scale,
           conv_w, conv_b, w_rg_a, b_rg_a, w_rg_x, b_rg_x, lru_lambda, w_proj_a, w_proj_b, w_out,
           ln1_g, ln1_b, w_router, b_router, w_e_gate, w_e_up, w_e_down, w_s_gate, w_s_up, w_s_down,
           ln2_g, ln2_b):
    B, S, D = x_prompt.shape
    B2, S2, _ = x_sample.shape
    L = w_in.shape[0]
    W_POOL = pool_scale.shape[1]
    W_LRU = conv_w.shape[2]
    E = w_router.shape[2]
    alpha = (2.0 * L) ** 0.25

    sb = math.gcd(S, S2)
    seq_lens = [S] * B + [S2] * B2
    T = sum(seq_lens)
    nblk = T // sb
    assert nblk <= 8
    x = jnp.concatenate([x_prompt.reshape(B * S, D), x_sample.reshape(B2 * S2, D)], axis=0)
    c_seq = jnp.concatenate([c_prompt, c_sample], axis=0)
    blk_seq = [q for q, n in enumerate(seq_lens) for _ in range(n // sb)]
    c8 = jnp.zeros((8, D), f32).at[:nblk].set(c_seq[jnp.array(blk_seq)])
    tq = _tile(sb, 512)
    tile_pos = jnp.array([p for n in seq_lens for p in range(0, n, tq)], i32)
    tile_len = jnp.array([n for n in seq_lens for _ in range(0, n, tq)], i32)

    w_in_b = w_in.astype(bf16)
    w_pool_b = w_pool.astype(bf16)
    w_pair, b_pair, lam_pair = _pair_heads(w_rg_a, w_rg_x, b_rg_a, b_rg_x, lru_lambda)
    w_pa_b, w_pb_b, w_out_b = w_proj_a.astype(bf16), w_proj_b.astype(bf16), w_out.astype(bf16)
    w_router_b = w_router.astype(bf16)
    w_eg_b, w_eu_b, w_ed_b = w_e_gate.astype(bf16), w_e_up.astype(bf16), w_e_down.astype(bf16)
    w_sg_b, w_su_b, w_sd_b = w_s_gate.astype(bf16), w_s_up.astype(bf16), w_s_down.astype(bf16)
    conv_b3 = conv_b.reshape(L, 1, W_LRU)
    pool_scale3 = pool_scale.reshape(L, 1, W_POOL)
    ln1_g3, ln1_b3 = ln1_g.reshape(L, 1, D), ln1_b.reshape(L, 1, D)
    ln2_g3, ln2_b3 = ln2_g.reshape(L, 1, D), ln2_b.reshape(L, 1, D)

    mod5 = _ada(c8, w_ada, b_ada).reshape(L, 8, 6, 1, D)

    tm_e = 256 if T * TOP_K >= 256 * E else 8
    nt_e = T * TOP_K // tm_e + E

    x, u = _ln_first(x, ln0_g, ln0_b, mod5, 0, sb)
    for l in range(L):
        z_pool = _matmul(u, w_in_b, l, 0, W_POOL, None, f32)
        z_lru = _matmul(u, w_in_b, l, W_POOL, W_LRU, None, f32)
        gz = _matmul(u, w_in_b, l, W_POOL + W_LRU, W_LRU, "gelu", bf16)
        gates = _matmul(u, w_in_b, l, W_POOL + 2 * W_LRU, 2 * D, "sigmoid", bf16)
        y_pool = _pool(z_pool, w_pool_b, pool_scale3, l, tile_pos, tile_len, tq)
        y_lru = _lru(z_lru, gz, conv_w, conv_b3, w_pair, b_pair, lam_pair, l, tile_pos, tile_len, tq)
        merged = _merge(y_pool, y_lru, w_pa_b, w_pb_b, gates, l)
        mix = _matmul(merged, w_out_b, l, 0, D, None, f32)
        x, u, u_f = _ln_mid(x, mix, ln1_g3, ln1_b3, mod5, l, sb, alpha, 2, 4, 3, True)
        idx8, wts8, pos8, counts = _router(u, w_router_b, b_router, l)
        te, rows, tok_s, dst_s, gate_s = _dispatch(idx8, wts8, pos8, counts[0], tm_e, nt_e)
        eo = _experts(u_f, w_eg_b, w_eu_b, w_ed_b, l, te, rows, tok_s, dst_s, gate_s, tm_e)
        shared = _shared_mlp(u, w_sg_b, w_su_b, w_sd_b, l)
        out = _combine(x, eo, shared, ln2_g3, ln2_b3, mod5, l, sb, alpha, l == L - 1)
        if l < L - 1:
            x, u = out
        else:
            x = out
    return (x[:B * S].reshape(B, S, D), x[B * S:].reshape(B2, S2, D))
```

```python
import functools
import math

import jax
import jax.numpy as jnp
from jax import lax
from jax.experimental import pallas as pl
from jax.experimental.pallas import tpu as pltpu

POOL_WINDOWS = (2, 4, 8, 16)
CONV_WIDTH = 4
LRU_C = 8.0
TOP_K = 8
ROUTE_SCALE = 2.5
LN_EPS = 1e-5
HALO = 8
LANES = 128
RADIX = 4
VMEM_LIMIT_V7X = 56 * 1024 * 1024

f32 = jnp.float32
bf16 = jnp.bfloat16
i32 = jnp.int32


def _tile(n, pref):
    if n <= pref:
        return n
    t = pref - pref % 8
    while t >= 8:
        if n % t == 0:
            return t
        t -= 8
    return n


def _params(*sem):
    return pltpu.CompilerParams(dimension_semantics=sem, vmem_limit_bytes=VMEM_LIMIT_V7X)


def _layer_norm(v, w, b):
    mu = jnp.mean(v, axis=-1, keepdims=True)
    d = v - mu
    var = jnp.mean(d * d, axis=-1, keepdims=True)
    return d * lax.rsqrt(var + LN_EPS) * w + b


def _ada_kernel(c_ref, w_ref, b_ref, o_ref):
    c = c_ref[...]
    s = c * jax.nn.sigmoid(c)
    o_ref[...] = jnp.dot(s.astype(bf16), w_ref[...].astype(bf16),
                         preferred_element_type=f32) + b_ref[...]


def _ada(c8, w_ada, b_ada):
    L, D, N = w_ada.shape
    tn = _tile(N, 512)
    return pl.pallas_call(
        _ada_kernel,
        out_shape=jax.ShapeDtypeStruct((L, 8, N), f32),
        grid=(L, N // tn),
        in_specs=[pl.BlockSpec((8, D), lambda l, j: (0, 0)),
                  pl.BlockSpec((None, D, tn), lambda l, j: (l, 0, j)),
                  pl.BlockSpec((None, 1, tn), lambda l, j: (l, 0, j))],
        out_specs=pl.BlockSpec((None, 8, tn), lambda l, j: (l, 0, j)),
        compiler_params=_params("parallel", "parallel"),
        name="ada_mod",
    )(c8, w_ada, b_ada.reshape(L, 1, N))


def _ln_first_kernel(x_ref, w_ref, b_ref, sc_ref, sh_ref, xo_ref, uo_ref):
    y = _layer_norm(x_ref[...], w_ref[...], b_ref[...])
    xo_ref[...] = y
    uo_ref[...] = (y * (1.0 + sc_ref[...]) + sh_ref[...]).astype(uo_ref.dtype)


def _ln_mid_kernel(x_ref, d_ref, g_ref, w_ref, b_ref, sc_ref, sh_ref, *out_refs, alpha):
    y = _layer_norm(alpha * x_ref[...] + g_ref[...] * d_ref[...], w_ref[...], b_ref[...])
    out_refs[0][...] = y
    u = y * (1.0 + sc_ref[...]) + sh_ref[...]
    for r in out_refs[1:]:
        r[...] = u.astype(r.dtype)


def _by_parity(step, cond, fn):
    for p in range(2):
        pl.when(jnp.logical_and(cond, step % 2 == p))(functools.partial(fn, p))


def _combine_kernel(sl_ref, sln_ref, x_ref, s_ref, g_ref, w_ref, b_ref, r_ref, *rest, alpha, last, tm, n):
    if last:
        eo_hbm, xo_ref, ebuf0, ebuf1, sem = rest
    else:
        sc_ref, sh_ref, eo_hbm, xo_ref, uo_ref, ebuf0, ebuf1, sem = rest
    i = pl.program_id(0)
    ebufs = (ebuf0, ebuf1)
    nrow = TOP_K * tm

    def gather_row(idx_ref, s, j):
        pltpu.make_async_copy(eo_hbm.at[pl.ds(idx_ref[0, 0, j], 1)],
                              ebufs[s].at[pl.ds(j, 1)], sem.at[s]).start()

    @pl.when(i == 0)
    def _():
        def first(j, c):
            gather_row(sl_ref, 0, j)
            return c
        lax.fori_loop(0, nrow, first, 0, unroll=8)

    def step(cur):
        @pl.when(i + 1 < n)
        def _():
            for j in range(nrow):
                gather_row(sln_ref, 1 - cur, j)

        pltpu.make_async_copy(eo_hbm.at[pl.ds(0, nrow)], ebufs[cur], sem.at[cur]).wait()
        ff = s_ref[...]
        for k in range(TOP_K):
            ff = ff + r_ref[:, k:k + 1] * ebufs[cur][k * tm:(k + 1) * tm, :]
        y = _layer_norm(alpha * x_ref[...] + g_ref[...] * ff, w_ref[...], b_ref[...])
        xo_ref[...] = y
        if not last:
            uo_ref[...] = (y * (1.0 + sc_ref[...]) + sh_ref[...]).astype(uo_ref.dtype)

    _by_parity(i, True, step)


def _mod_block_spec(D, l, which, blocks_per_tile_fn):
    return pl.BlockSpec((None, None, None, 1, D), lambda i: (l, blocks_per_tile_fn(i), which, 0, 0))


def _ln_first(x, w, b, mod5, l, sb):
    T, D = x.shape
    tm = _tile(sb, 128)
    blk = lambda i: (i * tm) // sb
    row = pl.BlockSpec((tm, D), lambda i: (i, 0))
    vec = pl.BlockSpec((1, D), lambda i: (0, 0))
    return pl.pallas_call(
        _ln_first_kernel,
        out_shape=(jax.ShapeDtypeStruct((T, D), f32), jax.ShapeDtypeStruct((T, D), bf16)),
        grid=(T // tm,),
        in_specs=[row, vec, vec, _mod_block_spec(D, l, 1, blk), _mod_block_spec(D, l, 0, blk)],
        out_specs=(row, row),
        compiler_params=_params("parallel"),
        name="ln_first",
    )(x, w.reshape(1, D), b.reshape(1, D), mod5, mod5)


def _ln_mid(x, delta, w, b, mod5, l, sb, alpha, g_idx, sc_idx, sh_idx, with_f32):
    T, D = x.shape
    tm = _tile(sb, 128)
    blk = lambda i: (i * tm) // sb
    row = pl.BlockSpec((tm, D), lambda i: (i, 0))
    vec = pl.BlockSpec((None, 1, D), lambda i: (l, 0, 0))
    outs = [jax.ShapeDtypeStruct((T, D), f32), jax.ShapeDtypeStruct((T, D), bf16)]
    if with_f32:
        outs.append(jax.ShapeDtypeStruct((T, D), f32))
    return pl.pallas_call(
        functools.partial(_ln_mid_kernel, alpha=alpha),
        out_shape=tuple(outs),
        grid=(T // tm,),
        in_specs=[row, row, _mod_block_spec(D, l, g_idx, blk), vec, vec,
                  _mod_block_spec(D, l, sc_idx, blk), _mod_block_spec(D, l, sh_idx, blk)],
        out_specs=tuple(row for _ in outs),
        compiler_params=_params("parallel"),
        name="ln_mid",
    )(x, delta, mod5, w, b, mod5, mod5)


def _combine(x, eo, slots_c, wts, shared, w, b, mod5, l, sb, alpha, last):
    T, D = x.shape
    n = slots_c.shape[0]
    tm = T // n
    blk = lambda i: (i * tm) // sb
    row = pl.BlockSpec((tm, D), lambda i: (i, 0))
    vec = pl.BlockSpec((None, 1, D), lambda i: (l, 0, 0))
    smem_tile = lambda f: pl.BlockSpec((1, 1, TOP_K * tm), f, memory_space=pltpu.SMEM)
    in_specs = [smem_tile(lambda i: (i, 0, 0)), smem_tile(lambda i: (jnp.minimum(i + 1, n - 1), 0, 0)),
                row, row, _mod_block_spec(D, l, 5, blk), vec, vec,
                pl.BlockSpec((tm, TOP_K), lambda i: (i, 0))]
    args = [slots_c, slots_c, x, shared, mod5, w, b, wts]
    if last:
        out_shape = jax.ShapeDtypeStruct((T, D), f32)
        out_specs = row
    else:
        in_specs += [_mod_block_spec(D, l + 1, 1, blk), _mod_block_spec(D, l + 1, 0, blk)]
        args += [mod5, mod5]
        out_shape = (jax.ShapeDtypeStruct((T, D), f32), jax.ShapeDtypeStruct((T, D), bf16))
        out_specs = (row, row)
    in_specs.append(pl.BlockSpec(memory_space=pl.ANY))
    args.append(eo)
    return pl.pallas_call(
        functools.partial(_combine_kernel, alpha=alpha, last=last, tm=tm, n=n),
        out_shape=out_shape,
        grid=(n,),
        in_specs=in_specs,
        out_specs=out_specs,
        scratch_shapes=[pltpu.VMEM((TOP_K * tm, D), f32), pltpu.VMEM((TOP_K * tm, D), f32),
                        pltpu.SemaphoreType.DMA((2,))],
        compiler_params=_params("arbitrary"),
        name="moe_combine_ln",
    )(*args)


def _mm_kernel(a_ref, b_ref, o_ref, *, act):
    acc = jnp.dot(a_ref[...], b_ref[...], preferred_element_type=f32)
    if act == "gelu":
        acc = jax.nn.gelu(acc)
    elif act == "sigmoid":
        acc = jax.nn.sigmoid(acc)
    o_ref[...] = acc.astype(o_ref.dtype)


def _matmul(a, w, l, col0, ncols, act, out_dtype, tm_pref=1024, tn_pref=1024):
    M, K = a.shape
    tm = _tile(M, tm_pref)
    tn = _tile(math.gcd(ncols, col0) if col0 else ncols, tn_pref)
    off = col0 // tn
    return pl.pallas_call(
        functools.partial(_mm_kernel, act=act),
        out_shape=jax.ShapeDtypeStruct((M, ncols), out_dtype),
        grid=(M // tm, ncols // tn),
        in_specs=[pl.BlockSpec((tm, K), lambda i, j: (i, 0)),
                  pl.BlockSpec((None, K, tn), lambda i, j: (l, 0, j + off))],
        out_specs=pl.BlockSpec((tm, tn), lambda i, j: (i, j)),
        compiler_params=_params("parallel", "parallel"),
        name="matmul_" + (act or "id"),
    )(a, w)


def _merge_kernel(ya_ref, yb_ref, wa_ref, wb_ref, ga_ref, gb_ref, o_ref):
    pa = jnp.dot(ya_ref[...], wa_ref[...], preferred_element_type=f32)
    pb = jnp.dot(yb_ref[...], wb_ref[...], preferred_element_type=f32)
    o_ref[...] = (ga_ref[...].astype(f32) * pa + gb_ref[...].astype(f32) * pb).astype(o_ref.dtype)


def _merge(ya, yb, wa, wb, gates, l):
    T, Ka = ya.shape
    Kb = yb.shape[1]
    D = wa.shape[2]
    tm = _tile(T, 512)
    tn = _tile(D, 512)
    nj = D // tn
    return pl.pallas_call(
        _merge_kernel,
        out_shape=jax.ShapeDtypeStruct((T, D), bf16),
        grid=(T // tm, nj),
        in_specs=[pl.BlockSpec((tm, Ka), lambda i, j: (i, 0)),
                  pl.BlockSpec((tm, Kb), lambda i, j: (i, 0)),
                  pl.BlockSpec((None, Ka, tn), lambda i, j: (l, 0, j)),
                  pl.BlockSpec((None, Kb, tn), lambda i, j: (l, 0, j)),
                  pl.BlockSpec((tm, tn), lambda i, j: (i, j)),
                  pl.BlockSpec((tm, tn), lambda i, j: (i, j + nj))],
        out_specs=pl.BlockSpec((tm, tn), lambda i, j: (i, j)),
        compiler_params=_params("parallel", "parallel"),
        name="merge_proj",
    )(ya, yb, wa, wb, gates, gates)


def _shared_kernel(u_ref, wg_ref, wu_ref, wd_ref, o_ref):
    u = u_ref[...]
    hg = jnp.dot(u, wg_ref[...], preferred_element_type=f32)
    hu = jnp.dot(u, wu_ref[...], preferred_element_type=f32)
    h = (hg * jax.nn.sigmoid(hg)) * hu
    o_ref[...] = jnp.dot(h.astype(bf16), wd_ref[...], preferred_element_type=f32)


def _shared_mlp(u, wg, wu, wd, l):
    T, D = u.shape
    F = wg.shape[2]
    tm = _tile(T, 256)
    return pl.pallas_call(
        _shared_kernel,
        out_shape=jax.ShapeDtypeStruct((T, D), f32),
        grid=(T // tm,),
        in_specs=[pl.BlockSpec((tm, D), lambda i: (i, 0)),
                  pl.BlockSpec((None, D, F), lambda i: (l, 0, 0)),
                  pl.BlockSpec((None, D, F), lambda i: (l, 0, 0)),
                  pl.BlockSpec((None, F, D), lambda i: (l, 0, 0))],
        out_specs=pl.BlockSpec((tm, D), lambda i: (i, 0)),
        compiler_params=_params("parallel"),
        name="shared_mlp",
    )(u, wg, wu, wd)


def _with_halo(z_ref, zp_ref, zn_ref, first, last):
    zp = jnp.where(first, 0.0, zp_ref[...])
    zn = jnp.where(last, 0.0, zn_ref[...])
    return jnp.concatenate([zp, z_ref[...], zn], axis=0)


def _pool_kernel(pos_ref, len_ref, z_ref, zp_ref, zn_ref, w_ref, s_ref, o_ref, *, tq):
    i = pl.program_id(0)
    pos0 = pos_ref[i]
    slen = len_ref[i]
    e = _with_halo(z_ref, zp_ref, zn_ref, pos0 == 0, pos0 + tq == slen)
    n = tq + 2 * HALO
    P = z_ref.shape[1] // len(POOL_WINDOWS)
    pos = pos0 + lax.broadcasted_iota(i32, (tq, 1), 0)
    for g, w in enumerate(POOL_WINDOWS):
        c = e[:, g * P:(g + 1) * P]
        span = 1
        while span < w:
            c = c + pltpu.roll(c, span, axis=0)
            span *= 2
        if w // 2 > 1:
            c = pltpu.roll(c, n - (w // 2 - 1), axis=0)
        s = c[HALO:HALO + tq]
        cnt = (jnp.minimum(pos + w // 2, slen) - jnp.maximum(pos - w // 2, 0)).astype(f32)
        d = s / cnt - z_ref[:, g * P:(g + 1) * P]
        y = jnp.dot(d.astype(bf16), w_ref[g], preferred_element_type=f32)
        o_ref[:, g * P:(g + 1) * P] = (y * s_ref[:, g * P:(g + 1) * P]).astype(o_ref.dtype)


def _halo_specs(tq, C, nrow8, col):
    r = tq // HALO
    return [pl.BlockSpec((tq, C), lambda *a: (col(*a)[0], col(*a)[1])),
            pl.BlockSpec((HALO, C), lambda *a: (jnp.maximum(col(*a)[0] * r - 1, 0), col(*a)[1])),
            pl.BlockSpec((HALO, C), lambda *a: (jnp.minimum((col(*a)[0] + 1) * r, nrow8 - 1), col(*a)[1]))]


def _pool(z, w_pool, pool_scale, l, tile_pos, tile_len, tq):
    T, W = z.shape
    G, P = w_pool.shape[1], w_pool.shape[2]
    grid_spec = pltpu.PrefetchScalarGridSpec(
        num_scalar_prefetch=2,
        grid=(T // tq,),
        in_specs=_halo_specs(tq, W, T // HALO, lambda i, *_: (i, 0)) + [
            pl.BlockSpec((None, G, P, P), lambda i, *_: (l, 0, 0, 0)),
            pl.BlockSpec((None, 1, W), lambda i, *_: (l, 0, 0))],
        out_specs=pl.BlockSpec((tq, W), lambda i, *_: (i, 0)),
    )
    return pl.pallas_call(
        functools.partial(_pool_kernel, tq=tq),
        out_shape=jax.ShapeDtypeStruct((T, W), bf16),
        grid_spec=grid_spec,
        compiler_params=_params("parallel"),
        name="pool_mixer",
    )(tile_pos, tile_len, z, z, z, w_pool, pool_scale)


def _slab_store(scr, val, start=0, stride=None):
    n = val.shape[0]
    for s in range(scr.shape[0]):
        rows = pl.ds(start, n) if stride is None else pl.ds(start, n, stride=stride)
        scr[s, rows, :] = val[:, s * LANES:(s + 1) * LANES]


def _slab_load(scr, n, start=0, stride=None):
    rows = pl.ds(start, n) if stride is None else pl.ds(start, n, stride=stride)
    return jnp.concatenate([scr[s, rows, :] for s in range(scr.shape[0])], axis=1)


def _lru_inputs(z_ref, zp_ref, zn_ref, cw_ref, cb_ref, w_ref, b_ref, lam_ref, e_scr, first, last, tq):
    _slab_store(e_scr, _with_halo(z_ref, zp_ref, zn_ref, first, last))
    q = tq // RADIX
    left = CONV_WIDTH // 2
    taps = {o: _slab_load(e_scr, q, HALO + o, RADIX) for o in range(-left, RADIX + CONV_WIDTH - 1 - left)}
    parts = []
    for r in range(RADIX):
        acc = None
        for k in range(CONV_WIDTH):
            tap = taps[r + k - left] * cw_ref[k:k + 1, :]
            acc = tap if acc is None else acc + tap
        parts.append(acc + cb_ref[...])
    xc = jnp.concatenate(parts, axis=0)
    C = xc.shape[1]
    pre = jnp.dot(xc.astype(bf16), w_ref[...], preferred_element_type=f32) + b_ref[...]
    r = jax.nn.sigmoid(pre[:, :C])
    ig = jax.nn.sigmoid(pre[:, C:])
    log_a = (-LRU_C) * r * jax.nn.softplus(-lam_ref[...])
    a = jnp.exp(log_a)
    th = jnp.tanh(log_a)
    u = jnp.sqrt(-2.0 * th / (1.0 - th)) * (ig * xc)
    return a, u


def _compose(a_parts, u_parts, order):
    A, U = {order[0]: a_parts[order[0]]}, {order[0]: u_parts[order[0]]}
    for prev, cur in zip(order, order[1:]):
        A[cur] = a_parts[cur] * A[prev]
        U[cur] = a_parts[cur] * U[prev] + u_parts[cur]
    return A, U


def _scan8_states(a, u, carry, reverse):
    r = lax.broadcasted_iota(i32, a.shape, 0)
    A, U = a, u
    for s in (1, 2, 4):
        shift, m = (8 - s, r < 8 - s) if reverse else (s, r >= s)
        As, Us = pltpu.roll(A, shift, axis=0), pltpu.roll(U, shift, axis=0)
        U = jnp.where(m, A * Us + U, U)
        A = jnp.where(m, A * As, A)
    after = U + A * carry
    if reverse:
        return jnp.where(r == 7, carry, pltpu.roll(after, 7, axis=0))
    return jnp.where(r == 0, carry, pltpu.roll(after, 1, axis=0))


def _entering_states(a, u, carry, scr, reverse):
    n = a.shape[0]
    if n == 8:
        return _scan8_states(a, u, carry, reverse)
    sa, su, sx = scr[n]
    _slab_store(sa, a)
    _slab_store(su, u)
    q = n // RADIX
    order = list(range(RADIX))[::-1] if reverse else list(range(RADIX))
    A, U = _compose([_slab_load(sa, q, r, RADIX) for r in range(RADIX)],
                    [_slab_load(su, q, r, RADIX) for r in range(RADIX)], order)
    c = _entering_states(A[order[-1]], U[order[-1]], carry, scr, reverse)
    _slab_store(sx, c, order[0], RADIX)
    for prev, cur in zip(order, order[1:]):
        _slab_store(sx, U[prev] + A[prev] * c, cur, RADIX)
    return _slab_load(sx, n)


def _scan_tile(a, u, carry, scr, h_scr, tq, reverse):
    q = tq // RADIX
    order = list(range(RADIX))[::-1] if reverse else list(range(RADIX))
    A, U = _compose([a[r * q:(r + 1) * q] for r in range(RADIX)],
                    [u[r * q:(r + 1) * q] for r in range(RADIX)], order)
    c = _entering_states(A[order[-1]], U[order[-1]], carry, scr, reverse)
    h = {r: U[r] + A[r] * c for r in range(RADIX)}
    for r in range(RADIX):
        _slab_store(h_scr, h[r], r, RADIX)
    return h[0][0:1, :] if reverse else h[RADIX - 1][q - 1:q, :]


def _lru_scratch(scr_refs, tq):
    sizes = []
    n = tq // RADIX
    while n > 8:
        sizes.append(n)
        n //= RADIX
    return {n: scr_refs[3 * j:3 * j + 3] for j, n in enumerate(sizes)}


def _lru_fwd_kernel(pos_ref, len_ref, z_ref, zp_ref, zn_ref, cw_ref, cb_ref, w_ref, b_ref, lam_ref,
                    h_ref, carry_ref, e_scr, h_scr, *scr_refs, tq):
    i = pl.program_id(1)
    pos0 = pos_ref[i]
    first = pos0 == 0
    a, u = _lru_inputs(z_ref, zp_ref, zn_ref, cw_ref, cb_ref, w_ref, b_ref, lam_ref, e_scr,
                       first, pos0 + tq == len_ref[i], tq)
    carry0 = jnp.where(first, 0.0, carry_ref[...])
    carry_ref[...] = _scan_tile(a, u, carry0, _lru_scratch(scr_refs, tq), h_scr, tq, False)
    h_ref[...] = _slab_load(h_scr, tq)


def _lru_bwd_kernel(pos_ref, len_ref, z_ref, zp_ref, zn_ref, cw_ref, cb_ref, w_ref, b_ref, lam_ref,
                    hf_ref, gz_ref, y_ref, carry_ref, e_scr, h_scr, *scr_refs, tq, nt):
    i = nt - 1 - pl.program_id(1)
    pos0 = pos_ref[i]
    last = pos0 + tq == len_ref[i]
    a, u = _lru_inputs(z_ref, zp_ref, zn_ref, cw_ref, cb_ref, w_ref, b_ref, lam_ref, e_scr,
                       pos0 == 0, last, tq)
    carry0 = jnp.where(last, 0.0, carry_ref[...])
    carry_ref[...] = _scan_tile(a, u, carry0, _lru_scratch(scr_refs, tq), h_scr, tq, True)
    y_ref[...] = (gz_ref[...].astype(f32) * (hf_ref[...] + _slab_load(h_scr, tq))).astype(y_ref.dtype)


def _lru(z, gz, conv_w, conv_b, w_pair, b_pair, lam_pair, l, tile_pos, tile_len, tq):
    T, W = z.shape
    NP, C = w_pair.shape[1], w_pair.shape[2]
    nt = T // tq
    assert C % LANES == 0 and 8 * RADIX ** round(math.log(tq // 8, RADIX)) == tq
    slab = lambda rows: pltpu.VMEM((C // LANES, rows, LANES), f32)
    scratch = [pltpu.VMEM((1, C), f32), slab(tq + 2 * HALO), slab(tq)]
    n = tq // RADIX
    while n > 8:
        scratch += [slab(n)] * 3
        n //= RADIX

    def specs(tile, d):
        col = lambda p, i, *_: (tile(i), p)
        return _halo_specs(tq, C, T // HALO, col) + [
            pl.BlockSpec((None, CONV_WIDTH, C), lambda p, i, *_: (l, 0, p)),
            pl.BlockSpec((None, 1, C), lambda p, i, *_: (l, 0, p)),
            pl.BlockSpec((None, None, C, 2 * C), lambda p, i, *_: (l, p, 0, d)),
            pl.BlockSpec((None, None, 1, 2 * C), lambda p, i, *_: (l, p, 0, d)),
            pl.BlockSpec((None, None, 1, C), lambda p, i, *_: (l, p, 0, d))]

    fwd_tile = lambda i: i
    h_fwd = pl.pallas_call(
        functools.partial(_lru_fwd_kernel, tq=tq),
        out_shape=jax.ShapeDtypeStruct((T, W), f32),
        grid_spec=pltpu.PrefetchScalarGridSpec(
            num_scalar_prefetch=2, grid=(NP, nt), in_specs=specs(fwd_tile, 0),
            out_specs=pl.BlockSpec((tq, C), lambda p, i, *_: (i, p)),
            scratch_shapes=scratch),
        compiler_params=_params("parallel", "arbitrary"),
        name="rglru_fwd",
    )(tile_pos, tile_len, z, z, z, conv_w, conv_b, w_pair, b_pair, lam_pair)

    bwd_tile = lambda i: nt - 1 - i
    row = pl.BlockSpec((tq, C), lambda p, i, *_: (nt - 1 - i, p))
    return pl.pallas_call(
        functools.partial(_lru_bwd_kernel, tq=tq, nt=nt),
        out_shape=jax.ShapeDtypeStruct((T, W), bf16),
        grid_spec=pltpu.PrefetchScalarGridSpec(
            num_scalar_prefetch=2, grid=(NP, nt), in_specs=specs(bwd_tile, 1) + [row, row],
            out_specs=row,
            scratch_shapes=scratch),
        compiler_params=_params("parallel", "arbitrary"),
        name="rglru_bwd",
    )(tile_pos, tile_len, z, z, z, conv_w, conv_b, w_pair, b_pair, lam_pair, h_fwd, gz)


def _router_kernel(u_ref, w_ref, b_ref, idx_ref, wts_ref, pos_ref, cnt_ref, carry_ref, *, tm, E):
    @pl.when(pl.program_id(0) == 0)
    def _():
        carry_ref[...] = jnp.zeros_like(carry_ref)

    logits = jnp.dot(u_ref[...], w_ref[...], preferred_element_type=f32)
    scores = jax.nn.sigmoid(logits)
    work = scores + b_ref[...]
    lane = lax.broadcasted_iota(i32, (tm, E), 1).astype(f32)
    lane_k = lax.broadcasted_iota(i32, (tm, TOP_K), 1)
    mask = jnp.zeros((tm, E), f32)
    idx8 = jnp.zeros((tm, TOP_K), f32)
    sel8 = jnp.zeros((tm, TOP_K), f32)
    onehots = []
    for k in range(TOP_K):
        m = jnp.max(work, axis=-1, keepdims=True)
        ik = jnp.min(jnp.where(work == m, lane, float(E)), axis=-1, keepdims=True)
        oh = lane == ik
        sk = jnp.sum(jnp.where(oh, scores, 0.0), axis=-1, keepdims=True)
        work = jnp.where(oh, -jnp.inf, work)
        mask = jnp.where(oh, 1.0, mask)
        idx8 = jnp.where(lane_k == k, ik, idx8)
        sel8 = jnp.where(lane_k == k, sk, sel8)
        onehots.append(oh)
    idx_ref[...] = idx8.astype(i32)
    wts_ref[...] = sel8 / jnp.sum(sel8, axis=-1, keepdims=True) * ROUTE_SCALE

    before = (lax.broadcasted_iota(i32, (tm, tm), 1) < lax.broadcasted_iota(i32, (tm, tm), 0))
    excl = jnp.dot(before.astype(bf16), mask.astype(bf16), preferred_element_type=f32)
    posd = excl + carry_ref[...]
    pos8 = jnp.zeros((tm, TOP_K), f32)
    for k in range(TOP_K):
        pk = jnp.sum(jnp.where(onehots[k], posd, 0.0), axis=-1, keepdims=True)
        pos8 = jnp.where(lane_k == k, pk, pos8)
    pos_ref[...] = pos8.astype(i32)
    total = carry_ref[...] + jnp.sum(mask, axis=0, keepdims=True)
    carry_ref[...] = total
    cnt_ref[...] = total.astype(i32)


def _router(u, w_router, b_router, l):
    T, D = u.shape
    E = w_router.shape[2]
    tm = _tile(T, 256)
    small = lambda n, dt: jax.ShapeDtypeStruct((T, n), dt)
    kspec = pl.BlockSpec((tm, TOP_K), lambda i: (i, 0))
    return pl.pallas_call(
        functools.partial(_router_kernel, tm=tm, E=E),
        out_shape=(small(TOP_K, i32), small(TOP_K, f32), small(TOP_K, i32),
                   jax.ShapeDtypeStruct((1, E), i32)),
        grid=(T // tm,),
        in_specs=[pl.BlockSpec((tm, D), lambda i: (i, 0)),
                  pl.BlockSpec((None, D, E), lambda i: (l, 0, 0)),
                  pl.BlockSpec((None, 1, E), lambda i: (l, 0, 0))],
        out_specs=(kspec, kspec, kspec, pl.BlockSpec((1, E), lambda i: (0, 0))),
        scratch_shapes=[pltpu.VMEM((1, E), f32)],
        compiler_params=_params("arbitrary"),
        name="router_topk",
    )(u, w_router, b_router.reshape(-1, 1, E))


def _expert_kernel(te_ref, rows_ref, tok_ref, tokn_ref, wg_ref, wu_ref, wd_ref,
                   x_hbm, o_ref, xbuf0, xbuf1, gsem, *, tm, nt):
    i = pl.program_id(0)
    xbufs = (xbuf0, xbuf1)

    def used(j):
        return jnp.logical_and(j >= 0, rows_ref[jnp.clip(j, 0, nt - 1)] > 0)

    def gather_row(idx_ref, s, r):
        pltpu.make_async_copy(x_hbm.at[pl.ds(idx_ref[0, 0, r], 1)],
                              xbufs[s].at[pl.ds(r, 1)], gsem.at[s]).start()

    def wait_gather(s):
        pltpu.make_async_copy(x_hbm.at[pl.ds(0, tm)], xbufs[s], gsem.at[s]).wait()

    @pl.when(i == 0)
    def _():
        def first(r, c):
            gather_row(tok_ref, 0, r)
            return c
        lax.fori_loop(0, tm, first, 0, unroll=8)

    _by_parity(i, jnp.logical_or(i == 0, used(i - 1)), wait_gather)

    def compute(cur):
        for r in range(tm):
            gather_row(tokn_ref, 1 - cur, r)
        x = xbufs[cur][...].astype(bf16)
        hg = jnp.dot(x, wg_ref[...], preferred_element_type=f32)
        hu = jnp.dot(x, wu_ref[...], preferred_element_type=f32)
        h = (hg * jax.nn.sigmoid(hg)) * hu
        o_ref[...] = jnp.dot(h.astype(bf16), wd_ref[...], preferred_element_type=f32)

    _by_parity(i, used(i), compute)

    @pl.when(jnp.logical_not(used(i)))
    def _():
        o_ref[...] = jnp.zeros(o_ref.shape, o_ref.dtype)


def _experts(u_f32, wg, wu, wd, l, tile_expert, tile_rows, tok_of_slot, tm):
    T, D = u_f32.shape
    F = wg.shape[3]
    nt = tile_expert.shape[0]
    smem_tile = lambda f: pl.BlockSpec((1, 1, tm), f, memory_space=pltpu.SMEM)
    wspec = lambda shape: pl.BlockSpec((None, None) + shape, lambda i, te, rows: (l, te[i], 0, 0))
    grid_spec = pltpu.PrefetchScalarGridSpec(
        num_scalar_prefetch=2,
        grid=(nt,),
        in_specs=[smem_tile(lambda i, te, rows: (i, 0, 0)),
                  smem_tile(lambda i, te, rows: (jnp.minimum(i + 1, nt - 1), 0, 0)),
                  wspec((D, F)), wspec((D, F)), wspec((F, D)),
                  pl.BlockSpec(memory_space=pl.ANY)],
        out_specs=pl.BlockSpec((tm, D), lambda i, te, rows: (i, 0)),
        scratch_shapes=[pltpu.VMEM((tm, D), f32), pltpu.VMEM((tm, D), f32),
                        pltpu.SemaphoreType.DMA((2,))],
    )
    tok3 = tok_of_slot.reshape(nt, 1, tm)
    return pl.pallas_call(
        functools.partial(_expert_kernel, tm=tm, nt=nt),
        out_shape=jax.ShapeDtypeStruct((nt * tm, D), f32),
        grid_spec=grid_spec,
        compiler_params=_params("arbitrary"),
        name="expert_mlp",
    )(tile_expert, tile_rows, tok3, tok3, wg, wu, wd, u_f32)


def _dispatch(idx8, pos8, counts, tm, nt, tm_c):
    T, K = idx8.shape
    E = counts.shape[0]
    tiles_e = (counts + tm - 1) // tm
    tile_end = jnp.cumsum(tiles_e)
    tile_start = tile_end - tiles_e
    slot = (tile_start * tm)[idx8] + pos8
    tok = jnp.broadcast_to(jnp.arange(T, dtype=i32)[:, None], (T, K))
    tok_of_slot = jnp.zeros((nt * tm,), i32).at[slot.reshape(-1)].set(tok.reshape(-1), unique_indices=True)
    tid = jnp.arange(nt, dtype=i32)
    total = tile_end[-1]
    te = jnp.sum(tile_end[None, :] <= jnp.minimum(tid, total - 1)[:, None], axis=1)
    te = jnp.minimum(te, E - 1).astype(i32)
    rows = jnp.clip(counts[te] - (tid - tile_start[te]) * tm, 0, tm)
    rows = jnp.where(tid < total, rows, 0).astype(i32)
    slots_c = slot.reshape(T // tm_c, tm_c, K).transpose(0, 2, 1).reshape(T // tm_c, 1, K * tm_c)
    return te, rows, tok_of_slot, slots_c


def _pair_heads(w_a, w_x, b_a, b_x, lam):
    L, _, H, hd, _ = w_a.shape
    NP = H // 2
    blocks = []
    for d in range(2):
        for wq in (w_a, w_x):
            wq = wq[:, d].astype(bf16).reshape(L, NP, 2, hd, hd)
            top = jnp.pad(wq[:, :, 0], ((0, 0), (0, 0), (0, 0), (0, hd)))
            bot = jnp.pad(wq[:, :, 1], ((0, 0), (0, 0), (0, 0), (hd, 0)))
            blocks.append(jnp.concatenate([top, bot], axis=2))
    w = jnp.concatenate(blocks, axis=3)
    b = jnp.stack([b_a, b_x], axis=2).reshape(L, 2, 2, NP, 2 * hd)
    b = b.transpose(0, 3, 1, 2, 4).reshape(L, NP, 1, 8 * hd)
    lam = lam.reshape(L, 2, NP, 2 * hd).transpose(0, 2, 1, 3).reshape(L, NP, 1, 4 * hd)
    return w, b, lam


def kernel(x_prompt, x_sample, c_prompt, c_sample, ln0_g, ln0_b, w_ada, b_ada, w_in, w_pool, pool_scale,
           conv_w, conv_b, w_rg_a, b_rg_a, w_rg_x, b_rg_x, lru_lambda, w_proj_a, w_proj_b, w_out,
           ln1_g, ln1_b, w_router, b_router, w_e_gate, w_e_up, w_e_down, w_s_gate, w_s_up, w_s_down,
           ln2_g, ln2_b):
    B, S, D = x_prompt.shape
    B2, S2, _ = x_sample.shape
    L = w_in.shape[0]
    W_POOL = pool_scale.shape[1]
    W_LRU = conv_w.shape[2]
    E = w_router.shape[2]
    alpha = (2.0 * L) ** 0.25

    sb = math.gcd(S, S2)
    seq_lens = [S] * B + [S2] * B2
    T = sum(seq_lens)
    nblk = T // sb
    assert nblk <= 8
    x = jnp.concatenate([x_prompt.reshape(B * S, D), x_sample.reshape(B2 * S2, D)], axis=0)
    c_seq = jnp.concatenate([c_prompt, c_sample], axis=0)
    blk_seq = [q for q, n in enumerate(seq_lens) for _ in range(n // sb)]
    c8 = jnp.zeros((8, D), f32).at[:nblk].set(c_seq[jnp.array(blk_seq)])
    tq = next(t for t in (512, 128, 32, 8) if sb % t == 0)
    tile_pos = jnp.array([p for n in seq_lens for p in range(0, n, tq)], i32)
    tile_len = jnp.array([n for n in seq_lens for _ in range(0, n, tq)], i32)

    w_in_b = w_in.astype(bf16)
    w_pool_b = w_pool.astype(bf16)
    w_pair, b_pair, lam_pair = _pair_heads(w_rg_a, w_rg_x, b_rg_a, b_rg_x, lru_lambda)
    w_pa_b, w_pb_b, w_out_b = w_proj_a.astype(bf16), w_proj_b.astype(bf16), w_out.astype(bf16)
    w_router_b = w_router.astype(bf16)
    w_eg_b, w_eu_b, w_ed_b = w_e_gate.astype(bf16), w_e_up.astype(bf16), w_e_down.astype(bf16)
    w_sg_b, w_su_b, w_sd_b = w_s_gate.astype(bf16), w_s_up.astype(bf16), w_s_down.astype(bf16)
    conv_b3 = conv_b.reshape(L, 1, W_LRU)
    pool_scale3 = pool_scale.reshape(L, 1, W_POOL)
    ln1_g3, ln1_b3 = ln1_g.reshape(L, 1, D), ln1_b.reshape(L, 1, D)
    ln2_g3, ln2_b3 = ln2_g.reshape(L, 1, D), ln2_b.reshape(L, 1, D)

    mod5 = _ada(c8, w_ada, b_ada).reshape(L, 8, 6, 1, D)

    tm_e = 256 if T * TOP_K >= 256 * E else 8
    nt_e = T * TOP_K // tm_e + E

    x, u = _ln_first(x, ln0_g, ln0_b, mod5, 0, sb)
    for l in range(L):
        z_pool = _matmul(u, w_in_b, l, 0, W_POOL, None, f32)
        z_lru = _matmul(u, w_in_b, l, W_POOL, W_LRU, None, f32)
        gz = _matmul(u, w_in_b, l, W_POOL + W_LRU, W_LRU, "gelu", bf16)
        gates = _matmul(u, w_in_b, l, W_POOL + 2 * W_LRU, 2 * D, "sigmoid", bf16)
        y_pool = _pool(z_pool, w_pool_b, pool_scale3, l, tile_pos, tile_len, tq)
        y_lru = _lru(z_lru, gz, conv_w, conv_b3, w_pair, b_pair, lam_pair, l, tile_pos, tile_len, tq)
        merged = _merge(y_pool, y_lru, w_pa_b, w_pb_b, gates, l)
        mix = _matmul(merged, w_out_b, l, 0, D, None, f32)
        x, u, u_f = _ln_mid(x, mix, ln1_g3, ln1_b3, mod5, l, sb, alpha, 2, 4, 3, True)
        idx8, wts8, pos8, counts = _router(u, w_router_b, b_router, l)
        te, rows, tok_s, slots_c = _dispatch(idx8, pos8, counts[0], tm_e, nt_e, _tile(sb, 64))
        eo = _experts(u_f, w_eg_b, w_eu_b, w_ed_b, l, te, rows, tok_s, tm_e)
        shared = _shared_mlp(u, w_sg_b, w_su_b, w_sd_b, l)
        out = _combine(x, eo, slots_c, wts8, shared, ln2_g3, ln2_b3, mod5, l, sb, alpha, l == L - 1)
        if l < L - 1:
            x, u = out
        else:
            x = out
    return (x[:B * S].reshape(B, S, D), x[B * S:].reshape(B2, S2, D))
```

```python
import functools
import math

import jax
import jax.numpy as jnp
from jax import lax
from jax.experimental import pallas as pl
from jax.experimental.pallas import tpu as pltpu

POOL_WINDOWS = (2, 4, 8, 16)
CONV_WIDTH = 4
LRU_C = 8.0
TOP_K = 8
ROUTE_SCALE = 2.5
LN_EPS = 1e-5
HALO = 8
LANES = 128
RADIX = 4
VMEM_LIMIT_V7X = 56 * 1024 * 1024

f32 = jnp.float32
bf16 = jnp.bfloat16
i32 = jnp.int32


def _tile(n, pref):
    if n <= pref:
        return n
    t = pref - pref % 8
    while t >= 8:
        if n % t == 0:
            return t
        t -= 8
    return n


def _params(*sem):
    return pltpu.CompilerParams(dimension_semantics=sem, vmem_limit_bytes=VMEM_LIMIT_V7X)


def _pack_halves(v):
    h = v.shape[1] // 2
    bits = lambda a: lax.bitcast_convert_type(a.astype(bf16).astype(f32), jnp.uint32)
    return lax.shift_right_logical(bits(v[:, :h]), jnp.uint32(16)) | bits(v[:, h:])


def _unpack_halves(w):
    lo = lax.bitcast_convert_type(lax.shift_left(w, jnp.uint32(16)), f32)
    hi = lax.bitcast_convert_type(w & jnp.uint32(0xFFFF0000), f32)
    return lo, hi


def _layer_norm(v, w, b):
    mu = jnp.mean(v, axis=-1, keepdims=True)
    d = v - mu
    var = jnp.mean(d * d, axis=-1, keepdims=True)
    return d * lax.rsqrt(var + LN_EPS) * w + b


def _ada_kernel(c_ref, w_ref, b_ref, o_ref):
    c = c_ref[...]
    s = c * jax.nn.sigmoid(c)
    o_ref[...] = jnp.dot(s.astype(bf16), w_ref[...].astype(bf16),
                         preferred_element_type=f32) + b_ref[...]


def _ada(c8, w_ada, b_ada):
    L, D, N = w_ada.shape
    tn = _tile(N, 512)
    return pl.pallas_call(
        _ada_kernel,
        out_shape=jax.ShapeDtypeStruct((L, 8, N), f32),
        grid=(L, N // tn),
        in_specs=[pl.BlockSpec((8, D), lambda l, j: (0, 0)),
                  pl.BlockSpec((None, D, tn), lambda l, j: (l, 0, j)),
                  pl.BlockSpec((None, 1, tn), lambda l, j: (l, 0, j))],
        out_specs=pl.BlockSpec((None, 8, tn), lambda l, j: (l, 0, j)),
        compiler_params=_params("parallel", "parallel"),
        name="ada_mod",
    )(c8, w_ada, b_ada.reshape(L, 1, N))


def _ln_first_kernel(x_ref, w_ref, b_ref, sc_ref, sh_ref, xo_ref, uo_ref):
    y = _layer_norm(x_ref[...], w_ref[...], b_ref[...])
    xo_ref[...] = y
    uo_ref[...] = (y * (1.0 + sc_ref[...]) + sh_ref[...]).astype(uo_ref.dtype)


def _ln_mid_kernel(x_ref, d_ref, g_ref, w_ref, b_ref, sc_ref, sh_ref, *out_refs, alpha):
    y = _layer_norm(alpha * x_ref[...] + g_ref[...] * d_ref[...], w_ref[...], b_ref[...])
    out_refs[0][...] = y
    u = y * (1.0 + sc_ref[...]) + sh_ref[...]
    out_refs[1][...] = u.astype(out_refs[1].dtype)
    if len(out_refs) > 2:
        out_refs[2][...] = _pack_halves(u)


RING = 3


def _by_ring(step, cond, fn):
    for s in range(RING):
        pl.when(jnp.logical_and(cond, step % RING == s))(functools.partial(fn, s))


def _combine_kernel(sl0_ref, sl1_ref, sl2_ref, x_ref, s_ref, g_ref, w_ref, b_ref, r_ref, *rest,
                    alpha, last, tm, n):
    if last:
        eo_hbm, xo_ref, ebuf0, ebuf1, ebuf2, sem = rest
    else:
        sc_ref, sh_ref, eo_hbm, xo_ref, uo_ref, ebuf0, ebuf1, ebuf2, sem = rest
    i = pl.program_id(0)
    ebufs = (ebuf0, ebuf1, ebuf2)
    nrow = TOP_K * tm

    def gather_row(idx_ref, s, j):
        pltpu.make_async_copy(eo_hbm.at[pl.ds(idx_ref[0, 0, j], 1)],
                              ebufs[s].at[pl.ds(j, 1)], sem.at[s]).start()

    def wait(s):
        pltpu.make_async_copy(eo_hbm.at[pl.ds(0, nrow)], ebufs[s], sem.at[s]).wait()

    @pl.when(i == 0)
    def _():
        for s, idx_ref in ((0, sl0_ref), (1, sl1_ref)):
            def first(j, c, s=s, idx_ref=idx_ref):
                gather_row(idx_ref, s, j)
                return c
            lax.fori_loop(0, nrow, first, 0, unroll=8)

    _by_ring(i, True, wait)

    def step(cur):
        for j in range(nrow):
            gather_row(sl2_ref, (cur + 2) % RING, j)
        lo = hi = None
        for k in range(TOP_K):
            e_lo, e_hi = _unpack_halves(ebufs[cur][k * tm:(k + 1) * tm, :])
            wk = r_ref[:, k:k + 1]
            lo = wk * e_lo if lo is None else lo + wk * e_lo
            hi = wk * e_hi if hi is None else hi + wk * e_hi
        ff = jnp.concatenate([lo, hi], axis=1) + s_ref[...]
        y = _layer_norm(alpha * x_ref[...] + g_ref[...] * ff, w_ref[...], b_ref[...])
        xo_ref[...] = y
        if not last:
            uo_ref[...] = (y * (1.0 + sc_ref[...]) + sh_ref[...]).astype(uo_ref.dtype)

    _by_ring(i, True, step)

    def drain(cur):
        wait((cur + 1) % RING)
        wait((cur + 2) % RING)

    _by_ring(i, i == n - 1, drain)


def _mod_block_spec(D, l, which, blocks_per_tile_fn):
    return pl.BlockSpec((None, None, None, 1, D), lambda i: (l, blocks_per_tile_fn(i), which, 0, 0))


def _ln_first(x, w, b, mod5, l, sb):
    T, D = x.shape
    tm = _tile(sb, 128)
    blk = lambda i: (i * tm) // sb
    row = pl.BlockSpec((tm, D), lambda i: (i, 0))
    vec = pl.BlockSpec((1, D), lambda i: (0, 0))
    return pl.pallas_call(
        _ln_first_kernel,
        out_shape=(jax.ShapeDtypeStruct((T, D), f32), jax.ShapeDtypeStruct((T, D), bf16)),
        grid=(T // tm,),
        in_specs=[row, vec, vec, _mod_block_spec(D, l, 1, blk), _mod_block_spec(D, l, 0, blk)],
        out_specs=(row, row),
        compiler_params=_params("parallel"),
        name="ln_first",
    )(x, w.reshape(1, D), b.reshape(1, D), mod5, mod5)


def _ln_mid(x, delta, w, b, mod5, l, sb, alpha, g_idx, sc_idx, sh_idx):
    T, D = x.shape
    tm = _tile(sb, 128)
    blk = lambda i: (i * tm) // sb
    row = pl.BlockSpec((tm, D), lambda i: (i, 0))
    vec = pl.BlockSpec((None, 1, D), lambda i: (l, 0, 0))
    outs = [jax.ShapeDtypeStruct((T, D), f32), jax.ShapeDtypeStruct((T, D), bf16),
            jax.ShapeDtypeStruct((T, D // 2), jnp.uint32)]
    return pl.pallas_call(
        functools.partial(_ln_mid_kernel, alpha=alpha),
        out_shape=tuple(outs),
        grid=(T // tm,),
        in_specs=[row, row, _mod_block_spec(D, l, g_idx, blk), vec, vec,
                  _mod_block_spec(D, l, sc_idx, blk), _mod_block_spec(D, l, sh_idx, blk)],
        out_specs=(row, row, pl.BlockSpec((tm, D // 2), lambda i: (i, 0))),
        compiler_params=_params("parallel"),
        name="ln_mid",
    )(x, delta, mod5, w, b, mod5, mod5)


def _combine(x, eo, slots_c, wts, shared, w, b, mod5, l, sb, alpha, last):
    T, D = x.shape
    n = slots_c.shape[0] - (RING - 1)
    tm = T // n
    blk = lambda i: (i * tm) // sb
    row = pl.BlockSpec((tm, D), lambda i: (i, 0))
    vec = pl.BlockSpec((None, 1, D), lambda i: (l, 0, 0))
    smem_tile = lambda a: pl.BlockSpec((1, 1, TOP_K * tm), lambda i: (i + a, 0, 0), memory_space=pltpu.SMEM)
    in_specs = [smem_tile(a) for a in range(RING)] + [
        row, row, _mod_block_spec(D, l, 5, blk), vec, vec, pl.BlockSpec((tm, TOP_K), lambda i: (i, 0))]
    args = [slots_c] * RING + [x, shared, mod5, w, b, wts]
    if last:
        out_shape = jax.ShapeDtypeStruct((T, D), f32)
        out_specs = row
    else:
        in_specs += [_mod_block_spec(D, l + 1, 1, blk), _mod_block_spec(D, l + 1, 0, blk)]
        args += [mod5, mod5]
        out_shape = (jax.ShapeDtypeStruct((T, D), f32), jax.ShapeDtypeStruct((T, D), bf16))
        out_specs = (row, row)
    in_specs.append(pl.BlockSpec(memory_space=pl.ANY))
    args.append(eo)
    return pl.pallas_call(
        functools.partial(_combine_kernel, alpha=alpha, last=last, tm=tm, n=n),
        out_shape=out_shape,
        grid=(n,),
        in_specs=in_specs,
        out_specs=out_specs,
        scratch_shapes=[pltpu.VMEM((TOP_K * tm, D // 2), jnp.uint32) for _ in range(RING)]
        + [pltpu.SemaphoreType.DMA((RING,))],
        compiler_params=_params("arbitrary"),
        name="moe_combine_ln",
    )(*args)


def _mm_kernel(a_ref, b_ref, o_ref, *, act):
    acc = jnp.dot(a_ref[...], b_ref[...], preferred_element_type=f32)
    if act == "gelu":
        acc = jax.nn.gelu(acc)
    elif act == "sigmoid":
        acc = jax.nn.sigmoid(acc)
    o_ref[...] = acc.astype(o_ref.dtype)


def _matmul(a, w, l, col0, ncols, act, out_dtype, tm_pref=1024, tn_pref=1024):
    M, K = a.shape
    tm = _tile(M, tm_pref)
    tn = _tile(math.gcd(ncols, col0) if col0 else ncols, tn_pref)
    off = col0 // tn
    return pl.pallas_call(
        functools.partial(_mm_kernel, act=act),
        out_shape=jax.ShapeDtypeStruct((M, ncols), out_dtype),
        grid=(M // tm, ncols // tn),
        in_specs=[pl.BlockSpec((tm, K), lambda i, j: (i, 0)),
                  pl.BlockSpec((None, K, tn), lambda i, j: (l, 0, j + off))],
        out_specs=pl.BlockSpec((tm, tn), lambda i, j: (i, j)),
        compiler_params=_params("parallel", "parallel"),
        name="matmul_" + (act or "id"),
    )(a, w)


def _merge_kernel(ya_ref, yb_ref, wa_ref, wb_ref, ga_ref, gb_ref, o_ref):
    pa = jnp.dot(ya_ref[...], wa_ref[...], preferred_element_type=f32)
    pb = jnp.dot(yb_ref[...], wb_ref[...], preferred_element_type=f32)
    o_ref[...] = (ga_ref[...].astype(f32) * pa + gb_ref[...].astype(f32) * pb).astype(o_ref.dtype)


def _merge(ya, yb, wa, wb, gates, l):
    T, Ka = ya.shape
    Kb = yb.shape[1]
    D = wa.shape[2]
    tm = _tile(T, 512)
    tn = _tile(D, 512)
    nj = D // tn
    return pl.pallas_call(
        _merge_kernel,
        out_shape=jax.ShapeDtypeStruct((T, D), bf16),
        grid=(T // tm, nj),
        in_specs=[pl.BlockSpec((tm, Ka), lambda i, j: (i, 0)),
                  pl.BlockSpec((tm, Kb), lambda i, j: (i, 0)),
                  pl.BlockSpec((None, Ka, tn), lambda i, j: (l, 0, j)),
                  pl.BlockSpec((None, Kb, tn), lambda i, j: (l, 0, j)),
                  pl.BlockSpec((tm, tn), lambda i, j: (i, j)),
                  pl.BlockSpec((tm, tn), lambda i, j: (i, j + nj))],
        out_specs=pl.BlockSpec((tm, tn), lambda i, j: (i, j)),
        compiler_params=_params("parallel", "parallel"),
        name="merge_proj",
    )(ya, yb, wa, wb, gates, gates)


def _shared_kernel(u_ref, wg_ref, wu_ref, wd_ref, o_ref):
    u = u_ref[...]
    hg = jnp.dot(u, wg_ref[...], preferred_element_type=f32)
    hu = jnp.dot(u, wu_ref[...], preferred_element_type=f32)
    h = (hg * jax.nn.sigmoid(hg)) * hu
    o_ref[...] = jnp.dot(h.astype(bf16), wd_ref[...], preferred_element_type=f32)


def _shared_mlp(u, wg, wu, wd, l):
    T, D = u.shape
    F = wg.shape[2]
    tm = _tile(T, 256)
    return pl.pallas_call(
        _shared_kernel,
        out_shape=jax.ShapeDtypeStruct((T, D), f32),
        grid=(T // tm,),
        in_specs=[pl.BlockSpec((tm, D), lambda i: (i, 0)),
                  pl.BlockSpec((None, D, F), lambda i: (l, 0, 0)),
                  pl.BlockSpec((None, D, F), lambda i: (l, 0, 0)),
                  pl.BlockSpec((None, F, D), lambda i: (l, 0, 0))],
        out_specs=pl.BlockSpec((tm, D), lambda i: (i, 0)),
        compiler_params=_params("parallel"),
        name="shared_mlp",
    )(u, wg, wu, wd)


def _with_halo(z_ref, zp_ref, zn_ref, first, last):
    zp = jnp.where(first, 0.0, zp_ref[...])
    zn = jnp.where(last, 0.0, zn_ref[...])
    return jnp.concatenate([zp, z_ref[...], zn], axis=0)


def _pool_kernel(pos_ref, len_ref, z_ref, zp_ref, zn_ref, w_ref, s_ref, o_ref, *, tq):
    i = pl.program_id(0)
    pos0 = pos_ref[i]
    slen = len_ref[i]
    e = _with_halo(z_ref, zp_ref, zn_ref, pos0 == 0, pos0 + tq == slen)
    n = tq + 2 * HALO
    P = z_ref.shape[1] // len(POOL_WINDOWS)
    pos = pos0 + lax.broadcasted_iota(i32, (tq, 1), 0)
    for g, w in enumerate(POOL_WINDOWS):
        c = e[:, g * P:(g + 1) * P]
        span = 1
        while span < w:
            c = c + pltpu.roll(c, span, axis=0)
            span *= 2
        if w // 2 > 1:
            c = pltpu.roll(c, n - (w // 2 - 1), axis=0)
        s = c[HALO:HALO + tq]
        cnt = (jnp.minimum(pos + w // 2, slen) - jnp.maximum(pos - w // 2, 0)).astype(f32)
        d = s / cnt - z_ref[:, g * P:(g + 1) * P]
        y = jnp.dot(d.astype(bf16), w_ref[g], preferred_element_type=f32)
        o_ref[:, g * P:(g + 1) * P] = (y * s_ref[:, g * P:(g + 1) * P]).astype(o_ref.dtype)


def _halo_specs(tq, C, nrow8, col):
    r = tq // HALO
    return [pl.BlockSpec((tq, C), lambda *a: (col(*a)[0], col(*a)[1])),
            pl.BlockSpec((HALO, C), lambda *a: (jnp.maximum(col(*a)[0] * r - 1, 0), col(*a)[1])),
            pl.BlockSpec((HALO, C), lambda *a: (jnp.minimum((col(*a)[0] + 1) * r, nrow8 - 1), col(*a)[1]))]


def _pool(z, w_pool, pool_scale, l, tile_pos, tile_len, tq):
    T, W = z.shape
    G, P = w_pool.shape[1], w_pool.shape[2]
    grid_spec = pltpu.PrefetchScalarGridSpec(
        num_scalar_prefetch=2,
        grid=(T // tq,),
        in_specs=_halo_specs(tq, W, T // HALO, lambda i, *_: (i, 0)) + [
            pl.BlockSpec((None, G, P, P), lambda i, *_: (l, 0, 0, 0)),
            pl.BlockSpec((None, 1, W), lambda i, *_: (l, 0, 0))],
        out_specs=pl.BlockSpec((tq, W), lambda i, *_: (i, 0)),
    )
    return pl.pallas_call(
        functools.partial(_pool_kernel, tq=tq),
        out_shape=jax.ShapeDtypeStruct((T, W), bf16),
        grid_spec=grid_spec,
        compiler_params=_params("parallel"),
        name="pool_mixer",
    )(tile_pos, tile_len, z, z, z, w_pool, pool_scale)


def _slab_store(scr, val, start=0, stride=None):
    n = val.shape[0]
    for s in range(scr.shape[0]):
        rows = pl.ds(start, n) if stride is None else pl.ds(start, n, stride=stride)
        scr[s, rows, :] = val[:, s * LANES:(s + 1) * LANES]


def _slab_load(scr, n, start=0, stride=None):
    rows = pl.ds(start, n) if stride is None else pl.ds(start, n, stride=stride)
    return jnp.concatenate([scr[s, rows, :] for s in range(scr.shape[0])], axis=1)


def _lru_inputs(z_ref, zp_ref, zn_ref, cw_ref, cb_ref, w_ref, b_ref, lam_ref, e_scr, first, last, tq):
    _slab_store(e_scr, _with_halo(z_ref, zp_ref, zn_ref, first, last))
    q = tq // RADIX
    left = CONV_WIDTH // 2
    taps = {o: _slab_load(e_scr, q, HALO + o, RADIX) for o in range(-left, RADIX + CONV_WIDTH - 1 - left)}
    parts = []
    for r in range(RADIX):
        acc = None
        for k in range(CONV_WIDTH):
            tap = taps[r + k - left] * cw_ref[k:k + 1, :]
            acc = tap if acc is None else acc + tap
        parts.append(acc + cb_ref[...])
    xc = jnp.concatenate(parts, axis=0)
    C = xc.shape[1]
    pre = jnp.dot(xc.astype(bf16), w_ref[...], preferred_element_type=f32) + b_ref[...]
    r = jax.nn.sigmoid(pre[:, :C])
    ig = jax.nn.sigmoid(pre[:, C:])
    log_a = (-LRU_C) * r * jax.nn.softplus(-lam_ref[...])
    a = jnp.exp(log_a)
    th = jnp.tanh(log_a)
    u = jnp.sqrt(-2.0 * th / (1.0 - th)) * (ig * xc)
    return a, u


def _compose(a_parts, u_parts, order):
    A, U = {order[0]: a_parts[order[0]]}, {order[0]: u_parts[order[0]]}
    for prev, cur in zip(order, order[1:]):
        A[cur] = a_parts[cur] * A[prev]
        U[cur] = a_parts[cur] * U[prev] + u_parts[cur]
    return A, U


def _scan8_states(a, u, carry, reverse):
    r = lax.broadcasted_iota(i32, a.shape, 0)
    A, U = a, u
    for s in (1, 2, 4):
        shift, m = (8 - s, r < 8 - s) if reverse else (s, r >= s)
        As, Us = pltpu.roll(A, shift, axis=0), pltpu.roll(U, shift, axis=0)
        U = jnp.where(m, A * Us + U, U)
        A = jnp.where(m, A * As, A)
    after = U + A * carry
    if reverse:
        return jnp.where(r == 7, carry, pltpu.roll(after, 7, axis=0))
    return jnp.where(r == 0, carry, pltpu.roll(after, 1, axis=0))


def _entering_states(a, u, carry, scr, reverse):
    n = a.shape[0]
    if n == 8:
        return _scan8_states(a, u, carry, reverse)
    sa, su, sx = scr[n]
    _slab_store(sa, a)
    _slab_store(su, u)
    q = n // RADIX
    order = list(range(RADIX))[::-1] if reverse else list(range(RADIX))
    A, U = _compose([_slab_load(sa, q, r, RADIX) for r in range(RADIX)],
                    [_slab_load(su, q, r, RADIX) for r in range(RADIX)], order)
    c = _entering_states(A[order[-1]], U[order[-1]], carry, scr, reverse)
    _slab_store(sx, c, order[0], RADIX)
    for prev, cur in zip(order, order[1:]):
        _slab_store(sx, U[prev] + A[prev] * c, cur, RADIX)
    return _slab_load(sx, n)


def _scan_tile(a, u, carry, scr, h_scr, tq, reverse):
    q = tq // RADIX
    order = list(range(RADIX))[::-1] if reverse else list(range(RADIX))
    A, U = _compose([a[r * q:(r + 1) * q] for r in range(RADIX)],
                    [u[r * q:(r + 1) * q] for r in range(RADIX)], order)
    c = _entering_states(A[order[-1]], U[order[-1]], carry, scr, reverse)
    h = {r: U[r] + A[r] * c for r in range(RADIX)}
    for r in range(RADIX):
        _slab_store(h_scr, h[r], r, RADIX)
    return h[0][0:1, :] if reverse else h[RADIX - 1][q - 1:q, :]


def _lru_scratch(scr_refs, tq):
    sizes = []
    n = tq // RADIX
    while n > 8:
        sizes.append(n)
        n //= RADIX
    return {n: scr_refs[3 * j:3 * j + 3] for j, n in enumerate(sizes)}


def _lru_fwd_kernel(pos_ref, len_ref, z_ref, zp_ref, zn_ref, cw_ref, cb_ref, w_ref, b_ref, lam_ref,
                    h_ref, carry_ref, e_scr, h_scr, *scr_refs, tq):
    i = pl.program_id(1)
    pos0 = pos_ref[i]
    first = pos0 == 0
    a, u = _lru_inputs(z_ref, zp_ref, zn_ref, cw_ref, cb_ref, w_ref, b_ref, lam_ref, e_scr,
                       first, pos0 + tq == len_ref[i], tq)
    carry0 = jnp.where(first, 0.0, carry_ref[...])
    carry_ref[...] = _scan_tile(a, u, carry0, _lru_scratch(scr_refs, tq), h_scr, tq, False)
    h_ref[...] = _slab_load(h_scr, tq)


def _lru_bwd_kernel(pos_ref, len_ref, z_ref, zp_ref, zn_ref, cw_ref, cb_ref, w_ref, b_ref, lam_ref,
                    hf_ref, gz_ref, y_ref, carry_ref, e_scr, h_scr, *scr_refs, tq, nt):
    i = nt - 1 - pl.program_id(1)
    pos0 = pos_ref[i]
    last = pos0 + tq == len_ref[i]
    a, u = _lru_inputs(z_ref, zp_ref, zn_ref, cw_ref, cb_ref, w_ref, b_ref, lam_ref, e_scr,
                       pos0 == 0, last, tq)
    carry0 = jnp.where(last, 0.0, carry_ref[...])
    carry_ref[...] = _scan_tile(a, u, carry0, _lru_scratch(scr_refs, tq), h_scr, tq, True)
    y_ref[...] = (gz_ref[...].astype(f32) * (hf_ref[...] + _slab_load(h_scr, tq))).astype(y_ref.dtype)


def _lru(z, gz, conv_w, conv_b, w_pair, b_pair, lam_pair, l, tile_pos, tile_len, tq):
    T, W = z.shape
    NP, C = w_pair.shape[1], w_pair.shape[2]
    nt = T // tq
    assert C % LANES == 0 and 8 * RADIX ** round(math.log(tq // 8, RADIX)) == tq
    slab = lambda rows: pltpu.VMEM((C // LANES, rows, LANES), f32)
    scratch = [pltpu.VMEM((1, C), f32), slab(tq + 2 * HALO), slab(tq)]
    n = tq // RADIX
    while n > 8:
        scratch += [slab(n)] * 3
        n //= RADIX

    def specs(tile, d):
        col = lambda p, i, *_: (tile(i), p)
        return _halo_specs(tq, C, T // HALO, col) + [
            pl.BlockSpec((None, CONV_WIDTH, C), lambda p, i, *_: (l, 0, p)),
            pl.BlockSpec((None, 1, C), lambda p, i, *_: (l, 0, p)),
            pl.BlockSpec((None, None, C, 2 * C), lambda p, i, *_: (l, p, 0, d)),
            pl.BlockSpec((None, None, 1, 2 * C), lambda p, i, *_: (l, p, 0, d)),
            pl.BlockSpec((None, None, 1, C), lambda p, i, *_: (l, p, 0, d))]

    fwd_tile = lambda i: i
    h_fwd = pl.pallas_call(
        functools.partial(_lru_fwd_kernel, tq=tq),
        out_shape=jax.ShapeDtypeStruct((T, W), f32),
        grid_spec=pltpu.PrefetchScalarGridSpec(
            num_scalar_prefetch=2, grid=(NP, nt), in_specs=specs(fwd_tile, 0),
            out_specs=pl.BlockSpec((tq, C), lambda p, i, *_: (i, p)),
            scratch_shapes=scratch),
        compiler_params=_params("parallel", "arbitrary"),
        name="rglru_fwd",
    )(tile_pos, tile_len, z, z, z, conv_w, conv_b, w_pair, b_pair, lam_pair)

    bwd_tile = lambda i: nt - 1 - i
    row = pl.BlockSpec((tq, C), lambda p, i, *_: (nt - 1 - i, p))
    return pl.pallas_call(
        functools.partial(_lru_bwd_kernel, tq=tq, nt=nt),
        out_shape=jax.ShapeDtypeStruct((T, W), bf16),
        grid_spec=pltpu.PrefetchScalarGridSpec(
            num_scalar_prefetch=2, grid=(NP, nt), in_specs=specs(bwd_tile, 1) + [row, row],
            out_specs=row,
            scratch_shapes=scratch),
        compiler_params=_params("parallel", "arbitrary"),
        name="rglru_bwd",
    )(tile_pos, tile_len, z, z, z, conv_w, conv_b, w_pair, b_pair, lam_pair, h_fwd, gz)


def _router_kernel(u_ref, w_ref, b_ref, idx_ref, wts_ref, pos_ref, cnt_ref, carry_ref, *, tm, E):
    @pl.when(pl.program_id(0) == 0)
    def _():
        carry_ref[...] = jnp.zeros_like(carry_ref)

    logits = jnp.dot(u_ref[...], w_ref[...], preferred_element_type=f32)
    scores = jax.nn.sigmoid(logits)
    work = scores + b_ref[...]
    lane = lax.broadcasted_iota(i32, (tm, E), 1).astype(f32)
    lane_k = lax.broadcasted_iota(i32, (tm, TOP_K), 1)
    mask = jnp.zeros((tm, E), f32)
    idx8 = jnp.zeros((tm, TOP_K), f32)
    sel8 = jnp.zeros((tm, TOP_K), f32)
    onehots = []
    for k in range(TOP_K):
        m = jnp.max(work, axis=-1, keepdims=True)
        ik = jnp.min(jnp.where(work == m, lane, float(E)), axis=-1, keepdims=True)
        oh = lane == ik
        sk = jnp.sum(jnp.where(oh, scores, 0.0), axis=-1, keepdims=True)
        work = jnp.where(oh, -jnp.inf, work)
        mask = jnp.where(oh, 1.0, mask)
        idx8 = jnp.where(lane_k == k, ik, idx8)
        sel8 = jnp.where(lane_k == k, sk, sel8)
        onehots.append(oh)
    idx_ref[...] = idx8.astype(i32)
    wts_ref[...] = sel8 / jnp.sum(sel8, axis=-1, keepdims=True) * ROUTE_SCALE

    before = (lax.broadcasted_iota(i32, (tm, tm), 1) < lax.broadcasted_iota(i32, (tm, tm), 0))
    excl = jnp.dot(before.astype(bf16), mask.astype(bf16), preferred_element_type=f32)
    posd = excl + carry_ref[...]
    pos8 = jnp.zeros((tm, TOP_K), f32)
    for k in range(TOP_K):
        pk = jnp.sum(jnp.where(onehots[k], posd, 0.0), axis=-1, keepdims=True)
        pos8 = jnp.where(lane_k == k, pk, pos8)
    pos_ref[...] = pos8.astype(i32)
    total = carry_ref[...] + jnp.sum(mask, axis=0, keepdims=True)
    carry_ref[...] = total
    cnt_ref[...] = total.astype(i32)


def _router(u, w_router, b_router, l):
    T, D = u.shape
    E = w_router.shape[2]
    tm = _tile(T, 256)
    small = lambda n, dt: jax.ShapeDtypeStruct((T, n), dt)
    kspec = pl.BlockSpec((tm, TOP_K), lambda i: (i, 0))
    return pl.pallas_call(
        functools.partial(_router_kernel, tm=tm, E=E),
        out_shape=(small(TOP_K, i32), small(TOP_K, f32), small(TOP_K, i32),
                   jax.ShapeDtypeStruct((1, E), i32)),
        grid=(T // tm,),
        in_specs=[pl.BlockSpec((tm, D), lambda i: (i, 0)),
                  pl.BlockSpec((None, D, E), lambda i: (l, 0, 0)),
                  pl.BlockSpec((None, 1, E), lambda i: (l, 0, 0))],
        out_specs=(kspec, kspec, kspec, pl.BlockSpec((1, E), lambda i: (0, 0))),
        scratch_shapes=[pltpu.VMEM((1, E), f32)],
        compiler_params=_params("arbitrary"),
        name="router_topk",
    )(u, w_router, b_router.reshape(-1, 1, E))


def _expert_kernel(te_ref, rows_ref, tok0_ref, tok1_ref, tok2_ref, wg_ref, wu_ref, wd_ref,
                   x_hbm, o_ref, xbuf0, xbuf1, xbuf2, gsem, *, tm, nt):
    i = pl.program_id(0)
    xbufs = (xbuf0, xbuf1, xbuf2)

    def used(j):
        return jnp.logical_and(j >= 0, rows_ref[jnp.clip(j, 0, nt - 1)] > 0)

    def gather_row(idx_ref, s, r):
        pltpu.make_async_copy(x_hbm.at[pl.ds(idx_ref[0, 0, r], 1)],
                              xbufs[s].at[pl.ds(r, 1)], gsem.at[s]).start()

    def wait_gather(s):
        pltpu.make_async_copy(x_hbm.at[pl.ds(0, tm)], xbufs[s], gsem.at[s]).wait()

    @pl.when(i == 0)
    def _():
        for s, idx_ref in ((0, tok0_ref), (1, tok1_ref)):
            def first(r, c, s=s, idx_ref=idx_ref):
                gather_row(idx_ref, s, r)
                return c
            lax.fori_loop(0, tm, first, 0, unroll=8)

    _by_ring(i, jnp.logical_or(i < 2, used(i - 2)), wait_gather)

    def compute(cur):
        for r in range(tm):
            gather_row(tok2_ref, (cur + 2) % RING, r)
        x_lo, x_hi = (v.astype(bf16) for v in _unpack_halves(xbufs[cur][...]))
        half = x_lo.shape[1]

        def proj(w_ref):
            return (jnp.dot(x_lo, w_ref[:half, :], preferred_element_type=f32)
                    + jnp.dot(x_hi, w_ref[half:, :], preferred_element_type=f32))

        hg, hu = proj(wg_ref), proj(wu_ref)
        h = (hg * jax.nn.sigmoid(hg)) * hu
        o_ref[...] = _pack_halves(jnp.dot(h.astype(bf16), wd_ref[...], preferred_element_type=f32))

    _by_ring(i, used(i), compute)

    @pl.when(jnp.logical_not(used(i)))
    def _():
        o_ref[...] = jnp.zeros(o_ref.shape, o_ref.dtype)


def _experts(u_pk, wg, wu, wd, l, tile_expert, tile_rows, tok_of_slot, tm):
    D = 2 * u_pk.shape[1]
    F = wg.shape[3]
    nt = tile_expert.shape[0]
    smem_tile = lambda f: pl.BlockSpec((1, 1, tm), f, memory_space=pltpu.SMEM)
    wspec = lambda shape: pl.BlockSpec((None, None) + shape, lambda i, te, rows: (l, te[i], 0, 0))
    grid_spec = pltpu.PrefetchScalarGridSpec(
        num_scalar_prefetch=2,
        grid=(nt,),
        in_specs=[smem_tile(lambda i, te, rows, a=a: (jnp.minimum(i + a, nt - 1), 0, 0)) for a in range(3)]
        + [wspec((D, F)), wspec((D, F)), wspec((F, D)), pl.BlockSpec(memory_space=pl.ANY)],
        out_specs=pl.BlockSpec((tm, D // 2), lambda i, te, rows: (i, 0)),
        scratch_shapes=[pltpu.VMEM((tm, D // 2), jnp.uint32) for _ in range(3)]
        + [pltpu.SemaphoreType.DMA((3,))],
    )
    tok3 = tok_of_slot.reshape(nt, 1, tm)
    return pl.pallas_call(
        functools.partial(_expert_kernel, tm=tm, nt=nt),
        out_shape=jax.ShapeDtypeStruct((nt * tm, D // 2), jnp.uint32),
        grid_spec=grid_spec,
        compiler_params=_params("arbitrary"),
        name="expert_mlp",
    )(tile_expert, tile_rows, tok3, tok3, tok3, wg, wu, wd, u_pk)


def _dispatch(idx8, pos8, counts, tm, nt, tm_c):
    T, K = idx8.shape
    E = counts.shape[0]
    tiles_e = (counts + tm - 1) // tm
    tile_end = jnp.cumsum(tiles_e)
    tile_start = tile_end - tiles_e
    slot = (tile_start * tm)[idx8] + pos8
    tok = jnp.broadcast_to(jnp.arange(T, dtype=i32)[:, None], (T, K))
    tok_of_slot = jnp.zeros((nt * tm,), i32).at[slot.reshape(-1)].set(tok.reshape(-1), unique_indices=True)
    tid = jnp.arange(nt, dtype=i32)
    total = tile_end[-1]
    te = jnp.sum(tile_end[None, :] <= jnp.minimum(tid, total - 1)[:, None], axis=1)
    te = jnp.minimum(te, E - 1).astype(i32)
    rows = jnp.clip(counts[te] - (tid - tile_start[te]) * tm, 0, tm)
    rows = jnp.where(tid < total, rows, 0).astype(i32)
    slots_c = slot.reshape(T // tm_c, tm_c, K).transpose(0, 2, 1).reshape(T // tm_c, 1, K * tm_c)
    slots_c = jnp.pad(slots_c, ((0, RING - 1), (0, 0), (0, 0)))
    return te, rows, tok_of_slot, slots_c


def _pair_heads(w_a, w_x, b_a, b_x, lam):
    L, _, H, hd, _ = w_a.shape
    NP = H // 2
    blocks = []
    for d in range(2):
        for wq in (w_a, w_x):
            wq = wq[:, d].astype(bf16).reshape(L, NP, 2, hd, hd)
            top = jnp.pad(wq[:, :, 0], ((0, 0), (0, 0), (0, 0), (0, hd)))
            bot = jnp.pad(wq[:, :, 1], ((0, 0), (0, 0), (0, 0), (hd, 0)))
            blocks.append(jnp.concatenate([top, bot], axis=2))
    w = jnp.concatenate(blocks, axis=3)
    b = jnp.stack([b_a, b_x], axis=2).reshape(L, 2, 2, NP, 2 * hd)
    b = b.transpose(0, 3, 1, 2, 4).reshape(L, NP, 1, 8 * hd)
    lam = lam.reshape(L, 2, NP, 2 * hd).transpose(0, 2, 1, 3).reshape(L, NP, 1, 4 * hd)
    return w, b, lam


def kernel(x_prompt, x_sample, c_prompt, c_sample, ln0_g, ln0_b, w_ada, b_ada, w_in, w_pool, pool_scale,
           conv_w, conv_b, w_rg_a, b_rg_a, w_rg_x, b_rg_x, lru_lambda, w_proj_a, w_proj_b, w_out,
           ln1_g, ln1_b, w_router, b_router, w_e_gate, w_e_up, w_e_down, w_s_gate, w_s_up, w_s_down,
           ln2_g, ln2_b):
    B, S, D = x_prompt.shape
    B2, S2, _ = x_sample.shape
    L = w_in.shape[0]
    W_POOL = pool_scale.shape[1]
    W_LRU = conv_w.shape[2]
    E = w_router.shape[2]
    alpha = (2.0 * L) ** 0.25

    sb = math.gcd(S, S2)
    seq_lens = [S] * B + [S2] * B2
    T = sum(seq_lens)
    nblk = T // sb
    assert nblk <= 8
    x = jnp.concatenate([x_prompt.reshape(B * S, D), x_sample.reshape(B2 * S2, D)], axis=0)
    c_seq = jnp.concatenate([c_prompt, c_sample], axis=0)
    blk_seq = [q for q, n in enumerate(seq_lens) for _ in range(n // sb)]
    c8 = jnp.zeros((8, D), f32).at[:nblk].set(c_seq[jnp.array(blk_seq)])
    tq = next(t for t in (512, 128, 32, 8) if sb % t == 0)
    tile_pos = jnp.array([p for n in seq_lens for p in range(0, n, tq)], i32)
    tile_len = jnp.array([n for n in seq_lens for _ in range(0, n, tq)], i32)

    w_in_b = w_in.astype(bf16)
    w_pool_b = w_pool.astype(bf16)
    w_pair, b_pair, lam_pair = _pair_heads(w_rg_a, w_rg_x, b_rg_a, b_rg_x, lru_lambda)
    w_pa_b, w_pb_b, w_out_b = w_proj_a.astype(bf16), w_proj_b.astype(bf16), w_out.astype(bf16)
    w_router_b = w_router.astype(bf16)
    w_eg_b, w_eu_b, w_ed_b = w_e_gate.astype(bf16), w_e_up.astype(bf16), w_e_down.astype(bf16)
    w_sg_b, w_su_b, w_sd_b = w_s_gate.astype(bf16), w_s_up.astype(bf16), w_s_down.astype(bf16)
    conv_b3 = conv_b.reshape(L, 1, W_LRU)
    pool_scale3 = pool_scale.reshape(L, 1, W_POOL)
    ln1_g3, ln1_b3 = ln1_g.reshape(L, 1, D), ln1_b.reshape(L, 1, D)
    ln2_g3, ln2_b3 = ln2_g.reshape(L, 1, D), ln2_b.reshape(L, 1, D)

    mod5 = _ada(c8, w_ada, b_ada).reshape(L, 8, 6, 1, D)

    tm_e = 256 if T * TOP_K >= 256 * E else 8
    nt_e = T * TOP_K // tm_e + E + 1

    x, u = _ln_first(x, ln0_g, ln0_b, mod5, 0, sb)
    for l in range(L):
        z_pool = _matmul(u, w_in_b, l, 0, W_POOL, None, f32)
        z_lru = _matmul(u, w_in_b, l, W_POOL, W_LRU, None, f32)
        gz = _matmul(u, w_in_b, l, W_POOL + W_LRU, W_LRU, "gelu", bf16)
        gates = _matmul(u, w_in_b, l, W_POOL + 2 * W_LRU, 2 * D, "sigmoid", bf16)
        y_pool = _pool(z_pool, w_pool_b, pool_scale3, l, tile_pos, tile_len, tq)
        y_lru = _lru(z_lru, gz, conv_w, conv_b3, w_pair, b_pair, lam_pair, l, tile_pos, tile_len, tq)
        merged = _merge(y_pool, y_lru, w_pa_b, w_pb_b, gates, l)
        mix = _matmul(merged, w_out_b, l, 0, D, None, f32)
        x, u, u_f = _ln_mid(x, mix, ln1_g3, ln1_b3, mod5, l, sb, alpha, 2, 4, 3)
        idx8, wts8, pos8, counts = _router(u, w_router_b, b_router, l)
        te, rows, tok_s, slots_c = _dispatch(idx8, pos8, counts[0], tm_e, nt_e, _tile(sb, 64))
        eo = _experts(u_f, w_eg_b, w_eu_b, w_ed_b, l, te, rows, tok_s, tm_e)
        shared = _shared_mlp(u, w_sg_b, w_su_b, w_sd_b, l)
        out = _combine(x, eo, slots_c, wts8, shared, ln2_g3, ln2_b3, mod5, l, sb, alpha, l == L - 1)
        if l < L - 1:
            x, u = out
        else:
            x = out
    return (x[:B * S].reshape(B, S, D), x[B * S:].reshape(B2, S2, D))
```
